```python
import math
import jax, jax.numpy as jnp
from jax import lax
import numpy as np

D_MODEL = 1024
BATCH = 32
SEQ = 2048
DEPTH = 2

N_META = 16
GRID_W = 64
CHUNK = 64
CONV_W = 5
D_FF = 2816
RMS_EPS = 1e-6

NA_HEADS = 8
NA_HEAD_DIM = 64
NA_WIN_H_MAX = 8
NA_WIN_W = 16
NA_QBLK_W = 16
NA_KBLK_W = NA_QBLK_W + NA_WIN_W
ML_HEADS = 4
ML_HEAD_DIM = 128
D_A = NA_HEADS * NA_HEAD_DIM
D_B = ML_HEADS * ML_HEAD_DIM
D_MIX = D_A + D_B
EV_IN_COLS = 3 * D_A + 4 * D_B + 4 * ML_HEADS

SSD_EXPAND = 2
D_INNER_C = SSD_EXPAND * D_MODEL
SSD_HEAD_DIM = 64
SSD_HEADS = D_INNER_C // SSD_HEAD_DIM
SSD_GROUPS = 8
SSD_HPG = SSD_HEADS // SSD_GROUPS
SSD_STATE = 128
SSD_CONV_CH = D_INNER_C + 2 * SSD_GROUPS * SSD_STATE
OD_IN_COLS = D_INNER_C + SSD_CONV_CH + 2 * SSD_HEADS

N_EVEN = (DEPTH + 1) // 2
N_ODD = DEPTH // 2

kernel_name = 'hybrid_bidir_natten_mlstm_ssd'


def rms_norm(x, w):
    xf = x.astype(jnp.float32)
    y = xf * lax.rsqrt(jnp.mean(xf * xf, axis=-1, keepdims=True) + RMS_EPS)
    return (y * w.astype(jnp.float32)).astype(x.dtype)


def swiglu(x, w_gate, w_up, w_down):
    return (jax.nn.silu(x @ w_gate) * (x @ w_up)) @ w_down


def centred_conv(x, w, b):
    c = x.shape[-1]
    p = (w.shape[0] - 1) // 2
    y = lax.conv_general_dilated(x, w[:, None, :].astype(x.dtype), window_strides=(1,),
                                 padding=[(p, p)], dimension_numbers=('NWC', 'WIO', 'NWC'),
                                 feature_group_count=c)
    return y + b.astype(x.dtype)


def neighbourhood_attention(q, k, v, rpb):
    bsz, t_len, n_h, dh = q.shape
    n_grid = t_len - N_META
    rows = n_grid // GRID_W
    win_h = min(NA_WIN_H_MAX, rows)
    n_cb = GRID_W // NA_QBLK_W
    n_key = win_h * NA_KBLK_W
    scale = dh ** -0.5
    qm, qg = q[:, :N_META], q[:, N_META:]
    km, kg = k[:, :N_META], k[:, N_META:]
    vm, vg = v[:, :N_META], v[:, N_META:]
    s_mm = jnp.einsum('bqhd,bkhd->bhqk', qm, km).astype(jnp.float32) * scale
    o_meta = jnp.einsum('bhqk,bkhd->bqhd', jax.nn.softmax(s_mm, axis=-1).astype(v.dtype), vm)
    r = np.arange(rows)
    key_rows = np.clip(r - win_h // 2, 0, rows - win_h)[:, None] + np.arange(win_h)
    cb = np.arange(n_cb)
    key_cols = np.clip(cb * NA_QBLK_W - NA_WIN_W // 2, 0, GRID_W - NA_KBLK_W)[:, None] + np.arange(NA_KBLK_W)
    q_cols = cb[:, None] * NA_QBLK_W + np.arange(NA_QBLK_W)
    win_c0 = np.clip(q_cols - NA_WIN_W // 2, 0, GRID_W - NA_WIN_W)
    col_ok = (key_cols[:, None, :] >= win_c0[..., None]) & (key_cols[:, None, :] < win_c0[..., None] + NA_WIN_W)
    key_idx = (key_rows[:, None, :, None] * GRID_W + key_cols[None, :, None, :]).reshape(rows, n_cb, n_key)
    dr = key_rows - r[:, None] + (NA_WIN_H_MAX - 1)
    dc = np.clip(key_cols[:, None, :] - q_cols[..., None], 1 - NA_WIN_W, NA_WIN_W - 1) + (NA_WIN_W - 1)
    bias = rpb.astype(jnp.float32)[:, dr[:, None, None, :, None], dc[None, :, :, None, :]]
    bias = jnp.where(col_ok[None, None, :, :, None, :], bias, -jnp.inf)
    bias = bias.reshape(n_h, rows, n_cb, NA_QBLK_W, n_key)
    bias = jnp.concatenate([bias, jnp.zeros((n_h, rows, n_cb, NA_QBLK_W, N_META), jnp.float32)], axis=-1)
    bias = jnp.moveaxis(bias, 1, 0)
    q_rows = jnp.moveaxis(qg.reshape(bsz, rows, n_cb, NA_QBLK_W, n_h, dh), 1, 0)

    def row_block(args):
        q_r, idx_r, bias_r = args
        k_r = kg[:, idx_r]
        v_r = vg[:, idx_r]
        s = jnp.concatenate([jnp.einsum('bcqhd,bckhd->bhcqk', q_r, k_r),
                             jnp.einsum('bcqhd,bmhd->bhcqm', q_r, km)], axis=-1)
        p = jax.nn.softmax(s.astype(jnp.float32) * scale + bias_r, axis=-1).astype(v.dtype)
        return (jnp.einsum('bhcqk,bckhd->bcqhd', p[..., :n_key], v_r)
                + jnp.einsum('bhcqm,bmhd->bcqhd', p[..., n_key:], vm))

    o_grid = lax.map(row_block, (q_rows, jnp.asarray(key_idx, jnp.int32), bias))
    o_grid = jnp.moveaxis(o_grid, 0, 1).reshape(bsz, n_grid, n_h * dh)
    return jnp.concatenate([o_meta.reshape(bsz, N_META, n_h * dh), o_grid], axis=1)


def mlstm_scan(q, k, v, log_i, log_f):
    bsz, n_h, tp, d = q.shape
    nc = tp // CHUNK

    def chunks(t):
        return jnp.moveaxis(t.reshape(t.shape[:2] + (nc, CHUNK) + t.shape[3:]), 2, 0)

    tril = jnp.tril(jnp.ones((CHUNK, CHUNK), bool))

    def step(carry, inp):
        c_st, n_st, m_st = carry
        qc, kc, vc, li, lf = inp
        g = jnp.cumsum(lf, axis=-1)
        dmat = jnp.where(tril, g[..., :, None] - g[..., None, :] + li[..., None, :], -jnp.inf)
        m_inter = g + m_st[..., None]
        m_t = jnp.maximum(m_inter, jnp.max(dmat, axis=-1))
        sw = jnp.exp(dmat - m_t[..., None]) * jnp.einsum('bhtd,bhsd->bhts', qc, kc)
        w_inter = jnp.exp(m_inter - m_t)
        num = jnp.einsum('bhts,bhsd->bhtd', sw, vc) + w_inter[..., None] * jnp.einsum('bhvk,bhtk->bhtv', c_st, qc)
        den = jnp.sum(sw, axis=-1) + w_inter * jnp.einsum('bhk,bhtk->bht', n_st, qc)
        h = num / jnp.maximum(jnp.abs(den), jnp.exp(-m_t))[..., None]
        g_last = g[..., -1]
        d_end = g_last[..., None] - g + li
        m_new = jnp.maximum(g_last + m_st, jnp.max(d_end, axis=-1))
        w_end = jnp.exp(d_end - m_new[..., None])
        w_old = jnp.exp(g_last + m_st - m_new)
        c_st = w_old[..., None, None] * c_st + jnp.einsum('bhs,bhsv,bhsk->bhvk', w_end, vc, kc)
        n_st = w_old[..., None] * n_st + jnp.einsum('bhs,bhsk->bhk', w_end, kc)
        return (c_st, n_st, m_new), h

    init = (jnp.zeros((bsz, n_h, d, d), q.dtype), jnp.zeros((bsz, n_h, d), q.dtype), jnp.zeros((bsz, n_h), q.dtype))
    _, h = lax.scan(step, init, (chunks(q), chunks(k), chunks(v), chunks(log_i), chunks(log_f)))
    return jnp.moveaxis(h, 0, 2).reshape(bsz, n_h, tp, d)


def mlstm_mixer(qb, kb, vb, ob, gates, conv_w, conv_b, gate_bias, norm_w):
    bsz, t_len, _ = qb.shape
    f32 = jnp.float32
    qk = jax.nn.silu(centred_conv(jnp.concatenate([qb, kb], axis=-1), conv_w, conv_b))
    q, k = jnp.split(qk, 2, axis=-1)

    def heads(t):
        return t.reshape(bsz, t_len, ML_HEADS, ML_HEAD_DIM).transpose(0, 2, 1, 3).astype(f32)

    q, k, v = heads(q), heads(k) * (ML_HEAD_DIM ** -0.5), heads(vb)
    g = (gates.astype(f32) + gate_bias.astype(f32)).transpose(0, 2, 1)
    i_f, f_f, i_b, f_b = jnp.split(g, 4, axis=1)
    pad = CHUNK - N_META

    def padt(t):
        return jnp.pad(t, [(0, 0), (0, 0), (pad, 0)] + [(0, 0)] * (t.ndim - 3))

    def flip(t):
        return jnp.flip(t, axis=2)

    q, k, v, i_f, f_f, i_b, f_b = (padt(t) for t in (q, k, v, i_f, f_f, i_b, f_b))
    h_fwd = mlstm_scan(q, k, v, i_f, jax.nn.log_sigmoid(f_f))
    h_bwd = flip(mlstm_scan(flip(q), flip(k), flip(v), flip(i_b), jax.nn.log_sigmoid(flip(f_b))))
    h = (h_fwd + h_bwd)[:, :, pad:]
    h = h * lax.rsqrt(jnp.mean(h * h, axis=-1, keepdims=True) + RMS_EPS)
    h = h.transpose(0, 2, 1, 3).reshape(bsz, t_len, D_B) * norm_w.astype(f32)
    return (h * jax.nn.sigmoid(ob.astype(f32))).astype(qb.dtype)


def even_mixer(hn, w_in, rpb, conv_w, conv_b, gate_bias, norm_w, w_out):
    bsz, t_len, _ = hn.shape
    proj = hn @ w_in
    splits = [int(s) for s in np.cumsum([D_A, D_A, D_A, D_B, D_B, D_B, D_B])]
    qa, ka, va, qb, kb, vb, ob, gates = jnp.split(proj, splits, axis=-1)

    def heads_a(t):
        return t.reshape(bsz, t_len, NA_HEADS, NA_HEAD_DIM)

    y_a = neighbourhood_attention(heads_a(qa), heads_a(ka), heads_a(va), rpb)
    y_b = mlstm_mixer(qb, kb, vb, ob, gates, conv_w, conv_b, gate_bias, norm_w)
    return jnp.concatenate([y_a, y_b.astype(y_a.dtype)], axis=-1) @ w_out


def ssd_scan(x, dt, a, bm, cm):
    bsz, tp = x.shape[:2]
    nc = tp // CHUNK

    def chunks(t):
        return jnp.moveaxis(t.reshape((bsz, nc, CHUNK) + t.shape[2:]), 1, 0)

    tril = jnp.tril(jnp.ones((CHUNK, CHUNK), bool))[None, :, :, None, None]

    def step(state, inp):
        xc, dtc, bc, cc = inp
        cum = jnp.cumsum(dtc * a, axis=1)
        seg = cum[:, :, None] - cum[:, None, :]
        decay = jnp.exp(jnp.where(tril, seg, -jnp.inf))
        cb = jnp.einsum('btgn,bsgn->btsg', cc, bc)
        y = jnp.einsum('btsgr,btsg,bsgr,bsgrp->btgrp', decay, cb, dtc, xc)
        y = y + jnp.einsum('btgn,bgrpn,btgr->btgrp', cc, state, jnp.exp(cum))
        cum_last = cum[:, -1]
        w = jnp.exp(cum_last[:, None] - cum) * dtc
        state = jnp.exp(cum_last)[..., None, None] * state + jnp.einsum('bsgr,bsgrp,bsgn->bgrpn', w, xc, bc)
        return state, y

    s0 = jnp.zeros((bsz,) + x.shape[2:] + (bm.shape[-1],), x.dtype)
    _, y = lax.scan(step, s0, (chunks(x), chunks(dt), chunks(bm), chunks(cm)))
    return jnp.moveaxis(y, 0, 1).reshape(x.shape)


def odd_mixer(hn, w_in, conv_w, conv_b, dt_bias, a_log, d_skip, norm_w, w_out):
    bsz, t_len, _ = hn.shape
    f32 = jnp.float32
    proj = hn @ w_in
    z, xbc, dt_raw = jnp.split(proj, [D_INNER_C, D_INNER_C + SSD_CONV_CH], axis=-1)
    xbc = jax.nn.silu(centred_conv(xbc, conv_w, conv_b))
    xs, bm, cm = jnp.split(xbc, [D_INNER_C, D_INNER_C + SSD_GROUPS * SSD_STATE], axis=-1)
    dt = jax.nn.softplus(dt_raw.astype(f32).reshape(bsz, t_len, 2, SSD_HEADS) + dt_bias.astype(f32))
    a = -jnp.exp(a_log.astype(f32)).reshape(2, SSD_GROUPS, SSD_HPG)
    xs = xs.astype(f32).reshape(bsz, t_len, SSD_GROUPS, SSD_HPG, SSD_HEAD_DIM)
    bm = bm.astype(f32).reshape(bsz, t_len, SSD_GROUPS, SSD_STATE)
    cm = cm.astype(f32).reshape(bsz, t_len, SSD_GROUPS, SSD_STATE)
    pad = CHUNK - N_META

    def padt(t):
        return jnp.pad(t, [(0, 0), (pad, 0)] + [(0, 0)] * (t.ndim - 2))

    def flip(t):
        return jnp.flip(t, axis=1)

    xp, bp, cp, dtp = padt(xs), padt(bm), padt(cm), padt(dt)
    dt_f = dtp[:, :, 0].reshape(bsz, -1, SSD_GROUPS, SSD_HPG)
    dt_b = dtp[:, :, 1].reshape(bsz, -1, SSD_GROUPS, SSD_HPG)
    y_f = ssd_scan(xp, dt_f, a[0], bp, cp)
    y_b = flip(ssd_scan(flip(xp), flip(dt_b), a[1], flip(bp), flip(cp)))
    y = (y_f + y_b)[:, pad:] + d_skip.astype(f32).reshape(SSD_GROUPS, SSD_HPG)[..., None] * xs
    y = y.reshape(bsz, t_len, D_INNER_C) * jax.nn.silu(z.astype(f32))
    yg = y.reshape(bsz, t_len, SSD_GROUPS, D_INNER_C // SSD_GROUPS)
    yg = yg * lax.rsqrt(jnp.mean(yg * yg, axis=-1, keepdims=True) + RMS_EPS)
    y = yg.reshape(bsz, t_len, D_INNER_C) * norm_w.astype(f32)
    return y.astype(hn.dtype) @ w_out


def setup_inputs(seed: int = 0) -> dict:
    key = jax.random.key(seed)
    ks = jax.random.split(key, 32)
    f32 = jnp.float32

    def nrm(k, shape, scale):
        return jax.random.normal(k, shape, f32) * scale

    def gain(k, shape):
        return 1.0 + 0.01 * jax.random.normal(k, shape, f32)

    gate_i = nrm(ks[17], (N_EVEN, 2, ML_HEADS), 0.1)
    gate_f = jnp.linspace(3.0, 6.0, ML_HEADS, dtype=f32) + nrm(ks[18], (N_EVEN, 2, ML_HEADS), 0.1)
    ml_gate_bias = jnp.stack([gate_i[:, 0], gate_f[:, 0], gate_i[:, 1], gate_f[:, 1]], axis=1).reshape(N_EVEN, 4 * ML_HEADS)
    dt0 = jnp.exp(jax.random.uniform(ks[24], (N_ODD, 2, SSD_HEADS), f32, math.log(1e-3), math.log(1e-1)))
    return {
        'x': jax.random.normal(ks[0], (BATCH, SEQ, D_MODEL), f32),
        'meta_tokens': nrm(ks[1], (N_META, D_MODEL), 1.0),
        'ffn1_norm': gain(ks[2], (DEPTH, D_MODEL)),
        'ffn1_gate': nrm(ks[3], (DEPTH, D_MODEL, D_FF), D_MODEL ** -0.5),
        'ffn1_up': nrm(ks[4], (DEPTH, D_MODEL, D_FF), D_MODEL ** -0.5),
        'ffn1_down': nrm(ks[5], (DEPTH, D_FF, D_MODEL), D_FF ** -0.5),
        'mix_norm': gain(ks[6], (DEPTH, D_MODEL)),
        'ffn2_norm': gain(ks[7], (DEPTH, D_MODEL)),
        'ffn2_gate': nrm(ks[8], (DEPTH, D_MODEL, D_FF), D_MODEL ** -0.5),
        'ffn2_up': nrm(ks[9], (DEPTH, D_MODEL, D_FF), D_MODEL ** -0.5),
        'ffn2_down': nrm(ks[10], (DEPTH, D_FF, D_MODEL), D_FF ** -0.5),
        'ev_w_in': nrm(ks[11], (N_EVEN, D_MODEL, EV_IN_COLS), D_MODEL ** -0.5),
        'na_rpb': nrm(ks[12], (N_EVEN, NA_HEADS, 2 * NA_WIN_H_MAX - 1, 2 * NA_WIN_W - 1), 0.1),
        'ml_conv_w': nrm(ks[13], (N_EVEN, CONV_W, 2 * D_B), CONV_W ** -0.5),
        'ml_conv_b': nrm(ks[14], (N_EVEN, 2 * D_B), 0.01),
        'ml_gate_bias': ml_gate_bias,
        'ml_norm_w': gain(ks[15], (N_EVEN, D_B)),
        'ev_w_out': nrm(ks[16], (N_EVEN, D_MIX, D_MODEL), D_MIX ** -0.5),
        'od_w_in': nrm(ks[19], (N_ODD, D_MODEL, OD_IN_COLS), D_MODEL ** -0.5),
        'ssd_conv_w': nrm(ks[20], (N_ODD, CONV_W, SSD_CONV_CH), CONV_W ** -0.5),
        'ssd_conv_b': nrm(ks[21], (N_ODD, SSD_CONV_CH), 0.01),
        'ssd_dt_bias': dt0 + jnp.log(-jnp.expm1(-dt0)),
        'ssd_a_log': jnp.log(jax.random.uniform(ks[25], (N_ODD, 2, SSD_HEADS), f32, 1.0, 16.0)),
        'ssd_d': gain(ks[26], (N_ODD, SSD_HEADS)),
        'ssd_norm_w': gain(ks[27], (N_ODD, D_INNER_C)),
        'od_w_out': nrm(ks[28], (N_ODD, D_INNER_C, D_MODEL), D_INNER_C ** -0.5),
        'final_norm': gain(ks[29], (D_MODEL,)),
    }


def reference(x, meta_tokens, ffn1_norm, ffn1_gate, ffn1_up, ffn1_down, mix_norm,
              ffn2_norm, ffn2_gate, ffn2_up, ffn2_down, ev_w_in, na_rpb, ml_conv_w,
              ml_conv_b, ml_gate_bias, ml_norm_w, ev_w_out, od_w_in, ssd_conv_w,
              ssd_conv_b, ssd_dt_bias, ssd_a_log, ssd_d, ssd_norm_w, od_w_out, final_norm):
    bsz = x.shape[0]
    meta = jnp.broadcast_to(meta_tokens.astype(x.dtype)[None], (bsz, N_META, D_MODEL))
    h = jnp.concatenate([meta, x], axis=1)
    for layer in range(DEPTH):
        h = h + 0.5 * swiglu(rms_norm(h, ffn1_norm[layer]), ffn1_gate[layer], ffn1_up[layer], ffn1_down[layer])
        hn = rms_norm(h, mix_norm[layer])
        if layer % 2 == 0:
            e = layer // 2
            h = h + even_mixer(hn, ev_w_in[e], na_rpb[e], ml_conv_w[e], ml_conv_b[e],
                               ml_gate_bias[e], ml_norm_w[e], ev_w_out[e])
        else:
            o = layer // 2
            h = h + odd_mixer(hn, od_w_in[o], ssd_conv_w[o], ssd_conv_b[o], ssd_dt_bias[o],
                              ssd_a_log[o], ssd_d[o], ssd_norm_w[o], od_w_out[o])
        h = h + 0.5 * swiglu(rms_norm(h, ffn2_norm[layer]), ffn2_gate[layer], ffn2_up[layer], ffn2_down[layer])
    return rms_norm(h, final_norm)[:, N_META:]
```

```python
import functools

import jax
import jax.numpy as jnp
import numpy as np
from jax import lax
from jax.experimental import pallas as pl
from jax.experimental.pallas import tpu as pltpu

F32 = jnp.float32
BF16 = jnp.bfloat16

D_MODEL = 1024
N_META = 16
GRID_W = 64
CHUNK = 64
PAD = CHUNK - N_META
CONV_W = 5
CONV_HALF = (CONV_W - 1) // 2
D_FF = 2816
RMS_EPS = 1e-6

NA_HEADS = 8
NA_HEAD_DIM = 64
NA_WIN_H = 8
NA_WIN_W = 16
NA_QBLK_W = 16
NA_KBLK_W = NA_QBLK_W + NA_WIN_W
ML_HEADS = 4
ML_HEAD_DIM = 128
D_A = NA_HEADS * NA_HEAD_DIM
D_B = ML_HEADS * ML_HEAD_DIM
EV_MAIN = 3 * D_A + 4 * D_B

SSD_HEAD_DIM = 64
SSD_HEADS = 32
SSD_GROUPS = 8
SSD_HPG = SSD_HEADS // SSD_GROUPS
SSD_STATE = 128
D_INNER = SSD_HEADS * SSD_HEAD_DIM
SSD_GW = SSD_HPG * SSD_HEAD_DIM
OD_MAIN = 2 * D_INNER + 2 * SSD_GROUPS * SSD_STATE

LANES = 128
FF_CHUNK = 256
ROW_TILE = 528
VMEM_LIMIT = 56 * 1024 * 1024


def _cparams(*sem):
    return pltpu.CompilerParams(dimension_semantics=sem, vmem_limit_bytes=VMEM_LIMIT)


def _resident(shape):
    nd = len(shape)
    return pl.BlockSpec(shape, lambda *_: (0,) * nd, pipeline_mode=pl.Buffered(1))


def _rms_rows(x, w_row):
    ms = jnp.mean(x * x, axis=-1, keepdims=True)
    return x * lax.rsqrt(ms + RMS_EPS) * w_row


def _sigmoid(x):
    return 1.0 / (1.0 + jnp.exp(-x))


def _silu(x):
    return x * _sigmoid(x)


def _softplus(x):
    return jnp.maximum(x, 0.0) + jnp.log(1.0 + jnp.exp(-jnp.abs(x)))


def _log_sigmoid(x):
    return -_softplus(-x)


def _nt(a, b):
    return lax.dot_general(a, b, (((1,), (1,)), ((), ())), preferred_element_type=F32)


def _tn(a, b):
    return lax.dot_general(a, b, (((0,), (0,)), ((), ())), preferred_element_type=F32)


def _split3(x):
    hi = x.astype(BF16)
    r1 = x - hi.astype(F32)
    mid = r1.astype(BF16)
    lo = (r1 - mid.astype(F32)).astype(BF16)
    return hi, mid, lo


def _cat3(x):
    return jnp.concatenate(_split3(x), axis=1)


def _ffn_kernel(x_ref, nw_ref, wgu_ref, wd_ref, o_ref, xn_ref, acc_ref):
    xn_ref[...] = _rms_rows(x_ref[...], nw_ref[...]).astype(BF16)
    acc_ref[...] = jnp.zeros_like(acc_ref)

    def body(f, carry):
        gu = jnp.dot(xn_ref[...], wgu_ref[f], preferred_element_type=F32)
        act = (_silu(gu[:, :FF_CHUNK]) * gu[:, FF_CHUNK:]).astype(BF16)
        acc_ref[...] += jnp.dot(act, wd_ref[f], preferred_element_type=F32)
        return carry

    lax.fori_loop(0, wgu_ref.shape[0], body, 0)
    o_ref[...] = x_ref[...] + 0.5 * acc_ref[...]


def _ffn(h2, norm_w, w_gate, w_up, w_down):
    rows = h2.shape[0]
    nf = D_FF // FF_CHUNK
    wg = w_gate.astype(BF16).reshape(D_MODEL, nf, FF_CHUNK)
    wu = w_up.astype(BF16).reshape(D_MODEL, nf, FF_CHUNK)
    wgu = jnp.concatenate([wg, wu], axis=-1).transpose(1, 0, 2)
    wd = w_down.astype(BF16).reshape(nf, FF_CHUNK, D_MODEL)
    row_spec = pl.BlockSpec((ROW_TILE, D_MODEL), lambda i: (i, 0))
    return pl.pallas_call(
        _ffn_kernel,
        grid=(rows // ROW_TILE,),
        in_specs=[row_spec, _resident((1, D_MODEL)), _resident(wgu.shape), _resident(wd.shape)],
        out_specs=row_spec,
        out_shape=jax.ShapeDtypeStruct(h2.shape, F32),
        scratch_shapes=[pltpu.VMEM((ROW_TILE, D_MODEL), BF16), pltpu.VMEM((ROW_TILE, D_MODEL), F32)],
        compiler_params=_cparams("parallel"),
    )(h2, norm_w.reshape(1, D_MODEL), wgu, wd)


def _norm_proj_kernel(x_ref, nw_ref, w_ref, ws_ref, o_ref, os_ref, xn_ref, *, col_chunk):
    xn_ref[...] = _rms_rows(x_ref[...], nw_ref[...]).astype(BF16)
    for c in range(w_ref.shape[1] // col_chunk):
        cols = slice(c * col_chunk, (c + 1) * col_chunk)
        o_ref[:, cols] = jnp.dot(xn_ref[...], w_ref[:, cols], preferred_element_type=F32).astype(BF16)
    os_ref[...] = jnp.dot(xn_ref[...], ws_ref[...], preferred_element_type=F32)


def _norm_proj(h2, norm_w, w_in, n_main):
    rows = h2.shape[0]
    w_main = w_in[:, :n_main].astype(BF16)
    n_small = w_in.shape[1] - n_main
    w_small = jnp.pad(w_in[:, n_main:], ((0, 0), (0, LANES - n_small))).astype(BF16)
    row_spec = pl.BlockSpec((ROW_TILE, D_MODEL), lambda i: (i, 0))
    return pl.pallas_call(
        functools.partial(_norm_proj_kernel, col_chunk=2 * FF_CHUNK),
        grid=(rows // ROW_TILE,),
        in_specs=[row_spec, _resident((1, D_MODEL)), _resident(w_main.shape), _resident(w_small.shape)],
        out_specs=[pl.BlockSpec((ROW_TILE, n_main), lambda i: (i, 0)),
                   pl.BlockSpec((ROW_TILE, LANES), lambda i: (i, 0))],
        out_shape=[jax.ShapeDtypeStruct((rows, n_main), BF16),
                   jax.ShapeDtypeStruct((rows, LANES), F32)],
        scratch_shapes=[pltpu.VMEM((ROW_TILE, D_MODEL), BF16)],
        compiler_params=_cparams("parallel"),
    )(h2, norm_w.reshape(1, D_MODEL), w_main, w_small)


def _out_proj_kernel(*refs):
    n = (len(refs) - 2) // 2
    h_ref, o_ref = refs[2 * n], refs[2 * n + 1]
    acc = h_ref[...]
    for y_ref, w_ref in zip(refs[:n], refs[n:2 * n]):
        acc = acc + jnp.dot(y_ref[...], w_ref[...], preferred_element_type=F32)
    o_ref[...] = acc


def _out_proj(ys, ws, h2):
    rows = h2.shape[0]
    row_spec = pl.BlockSpec((ROW_TILE, D_MODEL), lambda i: (i, 0))
    ws = [w.astype(BF16) for w in ws]
    return pl.pallas_call(
        _out_proj_kernel,
        grid=(rows // ROW_TILE,),
        in_specs=([pl.BlockSpec((ROW_TILE, y.shape[1]), lambda i: (i, 0)) for y in ys]
                  + [_resident(w.shape) for w in ws] + [row_spec]),
        out_specs=row_spec,
        out_shape=jax.ShapeDtypeStruct(h2.shape, F32),
        compiler_params=_cparams("parallel"),
    )(*ys, *ws, h2)


def _na_bias_table(rpb):
    qc = np.arange(GRID_W)[:, None]
    kc = np.arange(GRID_W)[None, :]
    dc = np.clip(kc - qc, 1 - NA_WIN_W, NA_WIN_W - 1) + (NA_WIN_W - 1)
    win_c0 = np.clip(qc - NA_WIN_W // 2, 0, GRID_W - NA_WIN_W)
    col_ok = (kc >= win_c0) & (kc < win_c0 + NA_WIN_W)
    dr = np.arange(NA_WIN_H)[:, None] + np.arange(NA_WIN_H)[None, :]
    tbl = rpb.astype(F32)[:, dr[:, :, None, None], dc[None, None, :, :]]
    tbl = jnp.where(col_ok[None, None, None], tbl, -jnp.inf)
    tbl = tbl.transpose(0, 1, 3, 2, 4).reshape(NA_HEADS, NA_WIN_H, GRID_W, NA_WIN_H * GRID_W)
    meta = jnp.zeros((NA_HEADS, NA_WIN_H, GRID_W, N_META), F32)
    return jnp.concatenate([tbl, meta], axis=-1)


def _natten_kernel(q_ref, k_ref, v_ref, bias_ref, o_ref, *, n_rows):
    scale = NA_HEAD_DIM ** -0.5
    g0 = PAD + N_META
    n_keys = NA_WIN_H * GRID_W

    def softmax_pv(s, v):
        m = jnp.max(s, axis=1, keepdims=True)
        p = jnp.exp(s - m)
        l = jnp.sum(p, axis=1, keepdims=True)
        return jnp.dot(p.astype(BF16), v, preferred_element_type=F32) / l

    def heads(fn):
        return jnp.concatenate(
            [fn(h, slice(h * NA_HEAD_DIM, (h + 1) * NA_HEAD_DIM)) for h in range(NA_HEADS)], axis=1)

    k_meta = k_ref[0, PAD:g0, :]
    v_meta = v_ref[0, PAD:g0, :]
    q_meta = q_ref[0, PAD:g0, :]
    o_ref[0, :PAD, :] = jnp.zeros((PAD, D_A), BF16)
    o_ref[0, PAD:g0, :] = heads(
        lambda h, c: softmax_pv(_nt(q_meta[:, c], k_meta[:, c]) * scale, v_meta[:, c])).astype(BF16)

    def row_body(r, carry):
        first = jnp.clip(r - NA_WIN_H // 2, 0, n_rows - NA_WIN_H)
        d0 = first - r + (NA_WIN_H - 1)
        q_rows = pl.ds(pl.multiple_of(g0 + r * GRID_W, GRID_W), GRID_W)
        k_rows = pl.ds(pl.multiple_of(g0 + first * GRID_W, GRID_W), n_keys)
        q = q_ref[0, q_rows, :]
        k = jnp.concatenate([k_ref[0, k_rows, :], k_meta], axis=0)
        v = jnp.concatenate([v_ref[0, k_rows, :], v_meta], axis=0)
        o_ref[0, q_rows, :] = heads(
            lambda h, c: softmax_pv(_nt(q[:, c], k[:, c]) * scale + bias_ref[h, d0], v[:, c])).astype(BF16)
        return carry

    lax.fori_loop(0, n_rows, row_body, 0)


def _natten(proj, bias_tbl):
    bsz, tp, _ = proj.shape
    n_rows = (tp - PAD - N_META) // GRID_W

    def col_block(j):
        return pl.BlockSpec((1, tp, D_A), lambda b: (b, 0, j))

    return pl.pallas_call(
        functools.partial(_natten_kernel, n_rows=n_rows),
        grid=(bsz,),
        in_specs=[col_block(0), col_block(1), col_block(2), _resident(bias_tbl.shape)],
        out_specs=pl.BlockSpec((1, tp, D_A), lambda b: (b, 0, 0)),
        out_shape=jax.ShapeDtypeStruct((bsz, tp, D_A), BF16),
        compiler_params=_cparams("parallel"),
    )(proj, proj, proj, bias_tbl)


def _fill_conv_src(src_ref, parts, tp):
    width = src_ref.shape[1]
    src_ref[0:8 + PAD, :] = jnp.zeros((8 + PAD, width), F32)
    src_ref[8 + tp:16 + tp, :] = jnp.zeros((8, width), F32)
    off = 0
    for p in parts:
        w = p.shape[2]
        src_ref[8 + PAD:8 + tp, off:off + w] = p[0, PAD:tp, :].astype(F32)
        off += w


def _scan_tri(reverse):
    t = lax.broadcasted_iota(jnp.int32, (CHUNK, CHUNK), 0)
    s = lax.broadcasted_iota(jnp.int32, (CHUNK, CHUNK), 1)
    return (s >= t) if reverse else (s <= t)


def _chunk_cumsum(x, reverse):
    tri = jnp.where(_scan_tri(reverse), 1.0, 0.0).astype(BF16)
    hi, mid, lo = _split3(x)
    return (jnp.dot(tri, hi, preferred_element_type=F32)
            + jnp.dot(tri, mid, preferred_element_type=F32)
            + jnp.dot(tri, lo, preferred_element_type=F32))


def _mlstm_kernel(q_ref, k_ref, v_ref, og_ref, gates_ref, cwq_ref, cwk_ref, cbq_ref, cbk_ref,
                  gb_ref, nw_ref, o_ref, src_ref, qc_ref, kc_ref, x_ref, lf_ref, y_ref,
                  st_ref, m_ref):
    tp = q_ref.shape[1]
    n_chunks = tp // CHUNK
    head = pl.program_id(1)
    hd = ML_HEAD_DIM

    _fill_conv_src(src_ref, (q_ref, k_ref), tp)
    row = lax.broadcasted_iota(jnp.int32, (tp, LANES), 0)
    x_all = jnp.where(row < PAD, 0.0, gates_ref[0] + gb_ref[...])
    x_ref[...] = x_all
    lf_ref[...] = _log_sigmoid(x_all)
    y_ref[...] = jnp.zeros_like(y_ref)
    st_ref[...] = jnp.zeros_like(st_ref)
    m_ref[...] = jnp.zeros_like(m_ref)

    crow = lax.broadcasted_iota(jnp.int32, (CHUNK, 2 * hd), 0)

    def conv_body(c, carry):
        rows = pl.ds(pl.multiple_of(c * CHUNK, CHUNK), CHUNK)
        w = jnp.concatenate([cwq_ref[...], cwk_ref[...]], axis=1)
        b = jnp.concatenate([cbq_ref[...], cbk_ref[...]], axis=1)
        x = src_ref[pl.ds(pl.multiple_of(c * CHUNK, CHUNK), CHUNK + 16), :]
        acc = b + w[0:1, :] * x[8 - CONV_HALF:8 - CONV_HALF + CHUNK]
        for j in range(1, CONV_W):
            lo = 8 - CONV_HALF + j
            acc = acc + w[j:j + 1, :] * x[lo:lo + CHUNK]
        qk = jnp.where(crow + c * CHUNK < PAD, 0.0, _silu(acc))
        qc_ref[rows, :] = qk[:, :hd].astype(BF16)
        kc_ref[rows, :] = (qk[:, hd:] * (hd ** -0.5)).astype(BF16)
        return carry

    lax.fori_loop(0, n_chunks, conv_body, 0)

    lane = lax.broadcasted_iota(jnp.int32, (CHUNK, LANES), 1)
    ones = jnp.ones((CHUNK, LANES), BF16)
    ones3 = jnp.concatenate([ones, ones, ones], axis=1)
    vrow = lax.broadcasted_iota(jnp.int32, (CHUNK, hd), 0)

    def chain(c, d):
        reverse = d == 1
        rows = pl.ds(pl.multiple_of(c * CHUNK, CHUNK), CHUNK)
        q = qc_ref[rows, :]
        k = kc_ref[rows, :]
        v = jnp.where(vrow + c * CHUNK < PAD, 0.0, v_ref[0, rows, :].astype(F32))
        xg = x_ref[rows, :]
        g_all = _chunk_cumsum(lf_ref[rows, :], reverse)
        gm = jnp.where(lane == 2 * ML_HEADS * d + ML_HEADS + head, g_all, 0.0)
        im = jnp.where(lane == 2 * ML_HEADS * d + head, xg, 0.0)
        g_col = jnp.sum(gm, axis=1, keepdims=True)
        i_col = jnp.sum(im, axis=1, keepdims=True)
        dmat = _nt(jnp.concatenate([_cat3(gm), ones3], axis=1),
                   jnp.concatenate([ones3, _cat3(im - gm)], axis=1))
        dmat = jnp.where(_scan_tri(reverse), dmat, -jnp.inf)
        m_st = m_ref[d, 0:1, 0:1]
        m_inter = g_col + m_st
        m_t = jnp.maximum(m_inter, jnp.max(dmat, axis=1, keepdims=True))
        sw = jnp.exp(dmat - m_t) * _nt(q, k)
        v1 = jnp.concatenate([v, jnp.ones((CHUNK, hd), F32)], axis=1)
        st = st_ref[d]
        tot = (jnp.dot(sw.astype(BF16), v1.astype(BF16), preferred_element_type=F32)
               + jnp.exp(m_inter - m_t) * jnp.dot(q, st.astype(BF16), preferred_element_type=F32))
        den = jnp.maximum(jnp.abs(tot[:, hd:]), jnp.exp(-m_t))
        y_ref[rows, :] += tot[:, :hd] / den
        g_last = g_col[0:1] if reverse else g_col[CHUNK - 1:CHUNK]
        d_end = g_last - g_col + i_col
        m_new = jnp.maximum(g_last + m_st, jnp.max(d_end, axis=0, keepdims=True))
        w_end = jnp.exp(d_end - m_new)
        st_ref[d] = jnp.exp(g_last + m_st - m_new) * st + _tn(k, (w_end * v1).astype(BF16))
        m_ref[d] = jnp.broadcast_to(m_new, m_ref.shape[1:])

    def body(i, carry):
        chain(i, 0)
        chain(n_chunks - 1 - i, 1)
        return carry

    lax.fori_loop(0, n_chunks, body, 0)

    y = y_ref[...]
    y = y * lax.rsqrt(jnp.mean(y * y, axis=1, keepdims=True) + RMS_EPS) * nw_ref[...]
    y = y * _sigmoid(og_ref[0].astype(F32))
    o_ref[0] = jnp.where(row < PAD, 0.0, y).astype(BF16)


def _mlstm(proj, gates, conv_w, conv_b, gate_bias, norm_w):
    bsz, tp, _ = proj.shape
    hd = ML_HEAD_DIM
    base = 3 * D_A // hd

    def col_block(j0):
        return pl.BlockSpec((1, tp, hd), lambda b, h: (b, 0, j0 + h))

    def par_block(nrow, j0):
        return pl.BlockSpec((nrow, hd), lambda b, h: (0, j0 + h))

    gb = jnp.pad(gate_bias.astype(F32), (0, LANES - gate_bias.shape[0])).reshape(1, LANES)
    return pl.pallas_call(
        _mlstm_kernel,
        grid=(bsz, ML_HEADS),
        in_specs=[col_block(base), col_block(base + ML_HEADS), col_block(base + 2 * ML_HEADS),
                  col_block(base + 3 * ML_HEADS),
                  pl.BlockSpec((1, tp, LANES), lambda b, h: (b, 0, 0)),
                  par_block(CONV_W, 0), par_block(CONV_W, ML_HEADS),
                  par_block(1, 0), par_block(1, ML_HEADS),
                  pl.BlockSpec((1, LANES), lambda b, h: (0, 0)),
                  par_block(1, 0)],
        out_specs=pl.BlockSpec((1, tp, hd), lambda b, h: (b, 0, h)),
        out_shape=jax.ShapeDtypeStruct((bsz, tp, D_B), BF16),
        scratch_shapes=[pltpu.VMEM((tp + 16, 2 * hd), F32),
                        pltpu.VMEM((tp, hd), BF16),
                        pltpu.VMEM((tp, hd), BF16),
                        pltpu.VMEM((tp, LANES), F32),
                        pltpu.VMEM((tp, LANES), F32),
                        pltpu.VMEM((tp, hd), F32),
                        pltpu.VMEM((2, hd, 2 * hd), F32),
                        pltpu.VMEM((2, 8, LANES), F32)],
        compiler_params=_cparams("parallel", "arbitrary"),
    )(proj, proj, proj, proj, gates, conv_w.astype(F32), conv_w.astype(F32),
      conv_b.astype(F32).reshape(1, -1), conv_b.astype(F32).reshape(1, -1), gb,
      norm_w.astype(F32).reshape(1, -1))


def _ssd_kernel(z_ref, x_ref, b_ref, c_ref, dt_ref, cwx_ref, cwb_ref, cwc_ref, cbx_ref, cbb_ref,
                cbc_ref, dtb_ref, alog_ref, dsk_ref, nw_ref, o_ref,
                src_ref, xs_ref, bc_ref, cc_ref, dts_ref, da_ref, y_ref, st_ref, sel_ref):
    tp = x_ref.shape[1]
    n_chunks = tp // CHUNK
    grp = pl.program_id(1)
    gw = SSD_GW
    ns = SSD_STATE

    _fill_conv_src(src_ref, (x_ref, b_ref, c_ref), tp)
    row = lax.broadcasted_iota(jnp.int32, (tp, LANES), 0)
    dt_all = jnp.where(row < PAD, 0.0, _softplus(dt_ref[0] + dtb_ref[...]))
    dts_ref[...] = dt_all
    da_ref[...] = dt_all * (-jnp.exp(alog_ref[...]))
    y_ref[...] = jnp.zeros_like(y_ref)
    st_ref[...] = jnp.zeros_like(st_ref)

    srow = lax.broadcasted_iota(jnp.int32, (gw, LANES), 0) // SSD_HEAD_DIM
    slane = lax.broadcasted_iota(jnp.int32, (gw, LANES), 1)
    for d in range(2):
        e = jnp.where(slane == SSD_HEADS * d + SSD_HPG * grp + srow, 1.0, 0.0).astype(BF16)
        sel_ref[d] = jnp.concatenate([e, e, e], axis=1)

    crow = lax.broadcasted_iota(jnp.int32, (CHUNK, gw + 2 * ns), 0)

    def conv_body(c, carry):
        rows = pl.ds(pl.multiple_of(c * CHUNK, CHUNK), CHUNK)
        w = jnp.concatenate([cwx_ref[...], cwb_ref[...], cwc_ref[...]], axis=1)
        b = jnp.concatenate([cbx_ref[...], cbb_ref[...], cbc_ref[...]], axis=1)
        x = src_ref[pl.ds(pl.multiple_of(c * CHUNK, CHUNK), CHUNK + 16), :]
        acc = b + w[0:1, :] * x[8 - CONV_HALF:8 - CONV_HALF + CHUNK]
        for j in range(1, CONV_W):
            lo = 8 - CONV_HALF + j
            acc = acc + w[j:j + 1, :] * x[lo:lo + CHUNK]
        a = jnp.where(crow + c * CHUNK < PAD, 0.0, _silu(acc))
        xs_ref[rows, :] = a[:, :gw]
        bc_ref[rows, :] = a[:, gw:gw + ns].astype(BF16)
        cc_ref[rows, :] = a[:, gw + ns:].astype(BF16)
        return carry

    lax.fori_loop(0, n_chunks, conv_body, 0)

    lane = lax.broadcasted_iota(jnp.int32, (CHUNK, LANES), 1)
    ones = jnp.ones((CHUNK, LANES), BF16)
    ones3 = jnp.concatenate([ones, ones, ones], axis=1)
    t4 = lax.broadcasted_iota(jnp.int32, (CHUNK, gw), 0)
    s4 = lax.broadcasted_iota(jnp.int32, (CHUNK, gw), 1)
    blk4 = s4 // SSD_HEAD_DIM
    s4 = s4 % SSD_HEAD_DIM

    def chain(c, d):
        reverse = d == 1
        rows = pl.ds(pl.multiple_of(c * CHUNK, CHUNK), CHUNK)
        lane0 = SSD_HEADS * d + SSD_HPG * grp
        mine = (lane >= lane0) & (lane < lane0 + SSD_HPG)
        cum = jnp.where(mine, _chunk_cumsum(da_ref[rows, :], reverse), 0.0)
        sel = sel_ref[d]
        neg = _split3(-cum)
        rhs = jnp.concatenate(
            [jnp.concatenate([sel[r * SSD_HEAD_DIM:(r + 1) * SSD_HEAD_DIM]]
                             + [jnp.where(lane == lane0 + r, p, jnp.zeros_like(p)) for p in neg], axis=1)
             for r in range(SSD_HPG)], axis=0)
        seg4 = _nt(jnp.concatenate([_cat3(cum), ones3], axis=1), rhs)
        tri4 = (s4 >= t4) if reverse else (s4 <= t4)
        decay4 = jnp.where(tri4, jnp.exp(seg4), 0.0)
        cum_last = cum[0:1] if reverse else cum[CHUNK - 1:CHUNK]
        ex = _nt(jnp.concatenate([_cat3(cum), _cat3(cum - cum_last),
                                  _cat3(jnp.broadcast_to(cum_last, (16, LANES))),
                                  _cat3(dts_ref[rows, :])], axis=0), sel)
        cum4 = ex[0:CHUNK]
        dl4 = ex[CHUNK:2 * CHUNK]
        cl4 = ex[2 * CHUNK:2 * CHUNK + 1]
        dt4 = ex[2 * CHUNK + 16:]
        bm = bc_ref[rows, :]
        cm = cc_ref[rows, :]
        cb4 = _nt(cm, jnp.concatenate([bm] * SSD_HPG, axis=0))
        xdt = xs_ref[rows, :] * dt4
        xdt_b = xdt.astype(BF16)
        xdiag = jnp.concatenate([jnp.where(blk4 == r, xdt_b, jnp.zeros_like(xdt_b))
                                 for r in range(SSD_HPG)], axis=0)
        st = st_ref[d]
        y = (jnp.dot((decay4 * cb4).astype(BF16), xdiag, preferred_element_type=F32)
             + jnp.exp(cum4) * jnp.dot(cm, st.astype(BF16), preferred_element_type=F32))
        y_ref[rows, :] += y
        st_ref[d] = jnp.exp(cl4) * st + _tn(bm, (jnp.exp(-dl4) * xdt).astype(BF16))

    def body(i, carry):
        chain(i, 0)
        chain(n_chunks - 1 - i, 1)
        return carry

    lax.fori_loop(0, n_chunks, body, 0)

    rowg = lax.broadcasted_iota(jnp.int32, (tp, gw), 0)
    y = (y_ref[...] + dsk_ref[...] * xs_ref[...]) * _silu(z_ref[0].astype(F32))
    y = y * lax.rsqrt(jnp.mean(y * y, axis=1, keepdims=True) + RMS_EPS) * nw_ref[...]
    o_ref[0] = jnp.where(rowg < PAD, 0.0, y).astype(BF16)


def _ssd(proj, dt_raw, conv_w, conv_b, dt_bias, a_log, d_skip, norm_w):
    bsz, tp, _ = proj.shape
    gw, ns = SSD_GW, SSD_STATE
    xb, bb, cb = D_INNER // gw, 2 * D_INNER // ns, 2 * D_INNER // ns + SSD_GROUPS

    def col_block(width, j0):
        return pl.BlockSpec((1, tp, width), lambda b, g: (b, 0, j0 + g))

    def par_block(nrow, width, j0):
        return pl.BlockSpec((nrow, width), lambda b, g: (0, j0 + g))

    def lane_row(v):
        v = v.astype(F32).reshape(-1)
        return jnp.pad(v, (0, LANES - v.shape[0])).reshape(1, LANES)

    cw = conv_w.astype(F32)
    cbias = conv_b.astype(F32).reshape(1, -1)
    xb_c, bb_c, cb_c = 0, D_INNER // ns, D_INNER // ns + SSD_GROUPS
    dsk = jnp.repeat(d_skip.astype(F32), SSD_HEAD_DIM).reshape(1, D_INNER)
    full_row = pl.BlockSpec((1, LANES), lambda b, g: (0, 0))
    return pl.pallas_call(
        _ssd_kernel,
        grid=(bsz, SSD_GROUPS),
        in_specs=[col_block(gw, 0), col_block(gw, xb), col_block(ns, bb), col_block(ns, cb),
                  pl.BlockSpec((1, tp, LANES), lambda b, g: (b, 0, 0)),
                  par_block(CONV_W, gw, xb_c), par_block(CONV_W, ns, bb_c), par_block(CONV_W, ns, cb_c),
                  par_block(1, gw, xb_c), par_block(1, ns, bb_c), par_block(1, ns, cb_c),
                  full_row, full_row, par_block(1, gw, 0), par_block(1, gw, 0)],
        out_specs=pl.BlockSpec((1, tp, gw), lambda b, g: (b, 0, g)),
        out_shape=jax.ShapeDtypeStruct((bsz, tp, D_INNER), BF16),
        scratch_shapes=[pltpu.VMEM((tp + 16, gw + 2 * ns), F32),
                        pltpu.VMEM((tp, gw), F32),
                        pltpu.VMEM((tp, ns), BF16),
                        pltpu.VMEM((tp, ns), BF16),
                        pltpu.VMEM((tp, LANES), F32),
                        pltpu.VMEM((tp, LANES), F32),
                        pltpu.VMEM((tp, gw), F32),
                        pltpu.VMEM((2, ns, gw), F32),
                        pltpu.VMEM((2, gw, 3 * LANES), BF16)],
        compiler_params=_cparams("parallel", "arbitrary"),
    )(proj, proj, proj, proj, dt_raw, cw, cw, cw, cbias, cbias, cbias,
      lane_row(dt_bias), lane_row(a_log), dsk, norm_w.astype(F32).reshape(1, -1))


def _final_kernel(x_ref, nw_ref, o_ref):
    seq = o_ref.shape[1]
    step = 256
    for r0 in range(0, seq, step):
        x = x_ref[0, PAD + N_META + r0:PAD + N_META + r0 + step, :]
        o_ref[0, r0:r0 + step, :] = _rms_rows(x, nw_ref[...])


def _final_norm(h3, norm_w, seq):
    bsz, tp, _ = h3.shape
    return pl.pallas_call(
        _final_kernel,
        grid=(bsz,),
        in_specs=[pl.BlockSpec((1, tp, D_MODEL), lambda b: (b, 0, 0)), _resident((1, D_MODEL))],
        out_specs=pl.BlockSpec((1, seq, D_MODEL), lambda b: (b, 0, 0)),
        out_shape=jax.ShapeDtypeStruct((bsz, seq, D_MODEL), F32),
        compiler_params=_cparams("parallel"),
    )(h3, norm_w.reshape(1, D_MODEL))


def _even_mixer(h2, bsz, tp, norm_w, w_in, rpb, conv_w, conv_b, gate_bias, ml_norm_w, w_out):
    proj, gates = _norm_proj(h2, norm_w, w_in, EV_MAIN)
    proj = proj.reshape(bsz, tp, EV_MAIN)
    y_a = _natten(proj, _na_bias_table(rpb))
    y_b = _mlstm(proj, gates.reshape(bsz, tp, LANES), conv_w, conv_b, gate_bias, ml_norm_w)
    return _out_proj([y_a.reshape(bsz * tp, D_A), y_b.reshape(bsz * tp, D_B)],
                     [w_out[:D_A], w_out[D_A:]], h2)


def _odd_mixer(h2, bsz, tp, norm_w, w_in, conv_w, conv_b, dt_bias, a_log, d_skip, ssd_norm_w, w_out):
    proj, dt_raw = _norm_proj(h2, norm_w, w_in, OD_MAIN)
    y = _ssd(proj.reshape(bsz, tp, OD_MAIN), dt_raw.reshape(bsz, tp, LANES), conv_w, conv_b,
             dt_bias, a_log, d_skip, ssd_norm_w)
    return _out_proj([y.reshape(bsz * tp, D_INNER)], [w_out], h2)


def kernel(x, meta_tokens, ffn1_norm, ffn1_gate, ffn1_up, ffn1_down, mix_norm, ffn2_norm, ffn2_gate, ffn2_up, ffn2_down, ev_w_in, na_rpb, ml_conv_w, ml_conv_b, ml_gate_bias, ml_norm_w, ev_w_out, od_w_in, ssd_conv_w, ssd_conv_b, ssd_dt_bias, ssd_a_log, ssd_d, ssd_norm_w, od_w_out, final_norm):
    bsz, seq, _ = x.shape
    tp = PAD + N_META + seq
    assert x.shape[2] == D_MODEL and seq % GRID_W == 0 and tp % ROW_TILE == 0
    assert seq // GRID_W >= NA_WIN_H
    head = jnp.concatenate([jnp.zeros((PAD, D_MODEL), F32), meta_tokens.astype(F32)], axis=0)
    h = jnp.concatenate([jnp.broadcast_to(head[None], (bsz, PAD + N_META, D_MODEL)), x], axis=1)
    h2 = h.reshape(bsz * tp, D_MODEL)
    for layer in range(ffn1_norm.shape[0]):
        h2 = _ffn(h2, ffn1_norm[layer], ffn1_gate[layer], ffn1_up[layer], ffn1_down[layer])
        i = layer // 2
        if layer % 2 == 0:
            h2 = _even_mixer(h2, bsz, tp, mix_norm[layer], ev_w_in[i], na_rpb[i], ml_conv_w[i],
                             ml_conv_b[i], ml_gate_bias[i], ml_norm_w[i], ev_w_out[i])
        else:
            h2 = _odd_mixer(h2, bsz, tp, mix_norm[layer], od_w_in[i], ssd_conv_w[i], ssd_conv_b[i],
                            ssd_dt_bias[i], ssd_a_log[i], ssd_d[i], ssd_norm_w[i], od_w_out[i])
        h2 = _ffn(h2, ffn2_norm[layer], ffn2_gate[layer], ffn2_up[layer], ffn2_down[layer])
    return _final_norm(h2.reshape(bsz, tp, D_MODEL), final_norm, seq)
```

```python
import functools

import jax
import jax.numpy as jnp
import numpy as np
from jax import lax
from jax.experimental import pallas as pl
from jax.experimental.pallas import tpu as pltpu

F32 = jnp.float32
BF16 = jnp.bfloat16

D_MODEL = 1024
N_META = 16
GRID_W = 64
CHUNK = 64
PAD = CHUNK - N_META
CONV_W = 5
CONV_HALF = (CONV_W - 1) // 2
D_FF = 2816
RMS_EPS = 1e-6

NA_HEADS = 8
NA_HEAD_DIM = 64
NA_WIN_H = 8
NA_WIN_W = 16
ML_HEADS = 4
ML_HEAD_DIM = 128
D_A = NA_HEADS * NA_HEAD_DIM
D_B = ML_HEADS * ML_HEAD_DIM
EV_MAIN = 3 * D_A + 4 * D_B

SSD_HEAD_DIM = 64
SSD_HEADS = 32
SSD_GROUPS = 8
SSD_HPG = SSD_HEADS // SSD_GROUPS
SSD_STATE = 128
D_INNER = SSD_HEADS * SSD_HEAD_DIM
SSD_GW = SSD_HPG * SSD_HEAD_DIM
OD_MAIN = 2 * D_INNER + 2 * SSD_GROUPS * SSD_STATE

LANES = 128
HALF = LANES // 2
FF_CHUNK = 256
ROW_TILE = 528
VMEM_LIMIT = 56 * 1024 * 1024


def _cparams(*sem):
    return pltpu.CompilerParams(dimension_semantics=sem, vmem_limit_bytes=VMEM_LIMIT)


def _resident(shape):
    nd = len(shape)
    return pl.BlockSpec(shape, lambda *_: (0,) * nd, pipeline_mode=pl.Buffered(1))


def _rms_rows(x, w_row):
    ms = jnp.mean(x * x, axis=-1, keepdims=True)
    return x * lax.rsqrt(ms + RMS_EPS) * w_row


def _sigmoid(x):
    return 1.0 / (1.0 + jnp.exp(-x))


def _silu(x):
    return x * _sigmoid(x)


def _softplus(x):
    return jnp.maximum(x, 0.0) + jnp.log(1.0 + jnp.exp(-jnp.abs(x)))


def _log_sigmoid(x):
    return -_softplus(-x)


def _nt(a, b):
    return lax.dot_general(a, b, (((1,), (1,)), ((), ())), preferred_element_type=F32)


def _tn(a, b):
    return lax.dot_general(a, b, (((0,), (0,)), ((), ())), preferred_element_type=F32)


def _ffn_kernel(x_ref, nw_ref, wgu_ref, wd_ref, o_ref, xn_ref, acc_ref):
    xn_ref[...] = _rms_rows(x_ref[...], nw_ref[...]).astype(BF16)
    acc_ref[...] = jnp.zeros_like(acc_ref)

    def body(f, carry):
        gu = jnp.dot(xn_ref[...], wgu_ref[f], preferred_element_type=F32)
        act = (_silu(gu[:, :FF_CHUNK]) * gu[:, FF_CHUNK:]).astype(BF16)
        acc_ref[...] += jnp.dot(act, wd_ref[f], preferred_element_type=F32)
        return carry

    lax.fori_loop(0, wgu_ref.shape[0], body, 0)
    o_ref[...] = x_ref[...] + 0.5 * acc_ref[...]


def _ffn(h2, norm_w, w_gate, w_up, w_down):
    rows = h2.shape[0]
    nf = D_FF // FF_CHUNK
    wg = w_gate.astype(BF16).reshape(D_MODEL, nf, FF_CHUNK)
    wu = w_up.astype(BF16).reshape(D_MODEL, nf, FF_CHUNK)
    wgu = jnp.concatenate([wg, wu], axis=-1).transpose(1, 0, 2)
    wd = w_down.astype(BF16).reshape(nf, FF_CHUNK, D_MODEL)
    row_spec = pl.BlockSpec((ROW_TILE, D_MODEL), lambda i: (i, 0))
    return pl.pallas_call(
        _ffn_kernel,
        grid=(rows // ROW_TILE,),
        in_specs=[row_spec, _resident((1, D_MODEL)), _resident(wgu.shape), _resident(wd.shape)],
        out_specs=row_spec,
        out_shape=jax.ShapeDtypeStruct(h2.shape, F32),
        scratch_shapes=[pltpu.VMEM((ROW_TILE, D_MODEL), BF16), pltpu.VMEM((ROW_TILE, D_MODEL), F32)],
        compiler_params=_cparams("parallel"),
    )(h2, norm_w.reshape(1, D_MODEL), wgu, wd)


def _norm_proj_kernel(x_ref, nw_ref, w_ref, ws_ref, o_ref, os_ref, xn_ref, *, col_chunk):
    xn_ref[...] = _rms_rows(x_ref[...], nw_ref[...]).astype(BF16)
    for c in range(w_ref.shape[1] // col_chunk):
        cols = slice(c * col_chunk, (c + 1) * col_chunk)
        o_ref[:, cols] = jnp.dot(xn_ref[...], w_ref[:, cols], preferred_element_type=F32).astype(BF16)
    os_ref[...] = jnp.dot(xn_ref[...], ws_ref[...], preferred_element_type=F32)


def _norm_proj(h2, norm_w, w_in, n_main):
    rows = h2.shape[0]
    w_main = w_in[:, :n_main].astype(BF16)
    n_small = w_in.shape[1] - n_main
    w_small = jnp.pad(w_in[:, n_main:], ((0, 0), (0, LANES - n_small))).astype(BF16)
    row_spec = pl.BlockSpec((ROW_TILE, D_MODEL), lambda i: (i, 0))
    return pl.pallas_call(
        functools.partial(_norm_proj_kernel, col_chunk=2 * FF_CHUNK),
        grid=(rows // ROW_TILE,),
        in_specs=[row_spec, _resident((1, D_MODEL)), _resident(w_main.shape), _resident(w_small.shape)],
        out_specs=[pl.BlockSpec((ROW_TILE, n_main), lambda i: (i, 0)),
                   pl.BlockSpec((ROW_TILE, LANES), lambda i: (i, 0))],
        out_shape=[jax.ShapeDtypeStruct((rows, n_main), BF16),
                   jax.ShapeDtypeStruct((rows, LANES), F32)],
        scratch_shapes=[pltpu.VMEM((ROW_TILE, D_MODEL), BF16)],
        compiler_params=_cparams("parallel"),
    )(h2, norm_w.reshape(1, D_MODEL), w_main, w_small)


def _out_proj_kernel(*refs):
    n = (len(refs) - 2) // 2
    h_ref, o_ref = refs[2 * n], refs[2 * n + 1]
    acc = h_ref[...]
    for y_ref, w_ref in zip(refs[:n], refs[n:2 * n]):
        acc = acc + jnp.dot(y_ref[...], w_ref[...], preferred_element_type=F32)
    o_ref[...] = acc


def _out_proj(ys, ws, h2):
    rows = h2.shape[0]
    row_spec = pl.BlockSpec((ROW_TILE, D_MODEL), lambda i: (i, 0))
    ws = [w.astype(BF16) for w in ws]
    return pl.pallas_call(
        _out_proj_kernel,
        grid=(rows // ROW_TILE,),
        in_specs=([pl.BlockSpec((ROW_TILE, y.shape[1]), lambda i: (i, 0)) for y in ys]
                  + [_resident(w.shape) for w in ws] + [row_spec]),
        out_specs=row_spec,
        out_shape=jax.ShapeDtypeStruct(h2.shape, F32),
        compiler_params=_cparams("parallel"),
    )(*ys, *ws, h2)


def _na_bias_table(rpb):
    qc = np.arange(GRID_W)[:, None]
    kc = np.arange(GRID_W)[None, :]
    dc = np.clip(kc - qc, 1 - NA_WIN_W, NA_WIN_W - 1) + (NA_WIN_W - 1)
    win_c0 = np.clip(qc - NA_WIN_W // 2, 0, GRID_W - NA_WIN_W)
    col_ok = (kc >= win_c0) & (kc < win_c0 + NA_WIN_W)
    onehot = (dc[:, :, None] == np.arange(2 * NA_WIN_W - 1)).astype(np.float32)
    toep = jnp.einsum('hdc,qkc->hdqk', rpb.astype(F32), onehot, precision=lax.Precision.HIGHEST)
    toep = jnp.where(col_ok[None, None], toep, -jnp.inf)
    tbl = jnp.stack([toep[:, d0:d0 + NA_WIN_H] for d0 in range(NA_WIN_H)], axis=1)
    tbl = tbl.transpose(0, 1, 3, 2, 4).reshape(NA_HEADS, NA_WIN_H, GRID_W, NA_WIN_H * GRID_W)
    meta = jnp.zeros((NA_HEADS, NA_WIN_H, GRID_W, N_META), F32)
    return jnp.concatenate([tbl, meta], axis=-1)


def _natten_kernel(q_ref, k_ref, v_ref, bias_ref, o_ref, *, n_rows):
    scale = NA_HEAD_DIM ** -0.5
    g0 = PAD + N_META
    n_keys = NA_WIN_H * GRID_W

    def softmax_pv(s, v):
        m = jnp.max(s, axis=1, keepdims=True)
        p = jnp.exp(s - m)
        l = jnp.sum(p, axis=1, keepdims=True)
        return jnp.dot(p.astype(BF16), v, preferred_element_type=F32) / l

    def heads(fn):
        return jnp.concatenate(
            [fn(h, slice(h * NA_HEAD_DIM, (h + 1) * NA_HEAD_DIM)) for h in range(NA_HEADS)], axis=1)

    k_meta = k_ref[0, PAD:g0, :]
    v_meta = v_ref[0, PAD:g0, :]
    q_meta = q_ref[0, PAD:g0, :]
    o_ref[0, :PAD, :] = jnp.zeros((PAD, D_A), BF16)
    o_ref[0, PAD:g0, :] = heads(
        lambda h, c: softmax_pv(_nt(q_meta[:, c], k_meta[:, c]) * scale, v_meta[:, c])).astype(BF16)

    def row_body(r, carry):
        first = jnp.clip(r - NA_WIN_H // 2, 0, n_rows - NA_WIN_H)
        d0 = first - r + (NA_WIN_H - 1)
        q_rows = pl.ds(pl.multiple_of(g0 + r * GRID_W, GRID_W), GRID_W)
        k_rows = pl.ds(pl.multiple_of(g0 + first * GRID_W, GRID_W), n_keys)
        q = q_ref[0, q_rows, :]
        k = jnp.concatenate([k_ref[0, k_rows, :], k_meta], axis=0)
        v = jnp.concatenate([v_ref[0, k_rows, :], v_meta], axis=0)
        o_ref[0, q_rows, :] = heads(
            lambda h, c: softmax_pv(_nt(q[:, c], k[:, c]) * scale + bias_ref[h, d0], v[:, c])).astype(BF16)
        return carry

    lax.fori_loop(0, n_rows, row_body, 0)


def _natten(proj, bias_tbl):
    bsz, tp, _ = proj.shape
    n_rows = (tp - PAD - N_META) // GRID_W

    def col_block(j):
        return pl.BlockSpec((1, tp, D_A), lambda b: (b, 0, j))

    return pl.pallas_call(
        functools.partial(_natten_kernel, n_rows=n_rows),
        grid=(bsz,),
        in_specs=[col_block(0), col_block(1), col_block(2), _resident(bias_tbl.shape)],
        out_specs=pl.BlockSpec((1, tp, D_A), lambda b: (b, 0, 0)),
        out_shape=jax.ShapeDtypeStruct((bsz, tp, D_A), BF16),
        compiler_params=_cparams("parallel"),
    )(proj, proj, proj, bias_tbl)


def _fill_conv_src(src_ref, parts, tp):
    width = src_ref.shape[1]
    src_ref[0:8 + PAD, :] = jnp.zeros((8 + PAD, width), F32)
    src_ref[8 + tp:16 + tp, :] = jnp.zeros((8, width), F32)
    off = 0
    for p in parts:
        w = p.shape[2]
        src_ref[8 + PAD:8 + tp, off:off + w] = p[0, PAD:tp, :].astype(F32)
        off += w


def _conv_silu(src_ref, w, b, c):
    x = src_ref[pl.ds(pl.multiple_of(c * CHUNK, CHUNK), CHUNK + 16), :]
    acc = b + w[0:1, :] * x[8 - CONV_HALF:8 - CONV_HALF + CHUNK]
    for j in range(1, CONV_W):
        lo = 8 - CONV_HALF + j
        acc = acc + w[j:j + 1, :] * x[lo:lo + CHUNK]
    row = lax.broadcasted_iota(jnp.int32, acc.shape, 0)
    return jnp.where(row + c * CHUNK < PAD, 0.0, _silu(acc))


def _chunk_rows(c):
    return pl.ds(pl.multiple_of(c * CHUNK, CHUNK), CHUNK)


def _scan_tri(shape, reverse):
    t = lax.broadcasted_iota(jnp.int32, shape, 0)
    s = lax.broadcasted_iota(jnp.int32, shape, 1) % CHUNK
    return (s >= t) if reverse else (s <= t)


def _cumsum_rows(x):
    row = lax.broadcasted_iota(jnp.int32, x.shape, 0)
    sh = 1
    while sh < CHUNK:
        x = x + jnp.where(row >= sh, pltpu.roll(x, sh, axis=0), 0.0)
        sh *= 2
    return x


def _pack3(x):
    hi = x.astype(BF16).astype(F32)
    r1 = x - hi
    mid = r1.astype(BF16).astype(F32)
    return jnp.concatenate([hi + pltpu.roll(mid, HALF, axis=1), r1 - mid], axis=1).astype(BF16)


def _pack2(x):
    hi = x.astype(BF16)
    return jnp.concatenate([hi, (x - hi.astype(F32)).astype(BF16)], axis=1)


def _selector(lane_of_row, terms):
    k = lax.broadcasted_iota(jnp.int32, lane_of_row.shape, 1)
    hit = (k == lane_of_row) | (k == lane_of_row + LANES)
    if terms == 3:
        hit = hit | (k == lane_of_row + HALF)
    return jnp.where(hit, 1.0, 0.0).astype(BF16)


def _mlstm_kernel(q_ref, k_ref, v_ref, og_ref, gates_ref, cwq_ref, cwk_ref, cbq_ref, cbk_ref,
                  gb_ref, nw_ref, o_ref, src_ref, qc_ref, kc_ref, x_ref, lf_ref, gf_ref,
                  stat_ref, m_ref, a_ref, wi_ref, em_ref, wv_ref, wo_ref, y_ref, st_ref):
    tp = q_ref.shape[1]
    n_chunks = tp // CHUNK
    head = pl.program_id(1)
    hd = ML_HEAD_DIM

    _fill_conv_src(src_ref, (q_ref, k_ref), tp)
    row = lax.broadcasted_iota(jnp.int32, (tp, LANES), 0)
    x_all = jnp.where(row < PAD, 0.0, gates_ref[0] + gb_ref[...])
    x_ref[...] = x_all
    lf_ref[...] = _log_sigmoid(x_all)
    st_ref[...] = jnp.zeros_like(st_ref)
    y_ref[...] = jnp.zeros_like(y_ref)

    conv_w = jnp.concatenate([cwq_ref[...], cwk_ref[...]], axis=1)
    conv_b = jnp.concatenate([cbq_ref[...], cbk_ref[...]], axis=1)

    def loop_a(c, carry):
        rows = _chunk_rows(c)
        qk = _conv_silu(src_ref, conv_w, conv_b, c)
        qc_ref[rows, :] = qk[:, :hd].astype(BF16)
        kc_ref[rows, :] = (qk[:, hd:] * (hd ** -0.5)).astype(BF16)
        lf = lf_ref[rows, :]
        gf = _cumsum_rows(lf)
        gf_ref[rows, :] = gf
        xi = pltpu.roll(x_ref[rows, :], ML_HEADS, axis=1)
        tot = gf[CHUNK - 1:CHUNK]
        gb = tot - gf + lf
        stat_ref[c] = jnp.concatenate(
            [tot, jnp.max(tot - gf + xi, axis=0, keepdims=True),
             tot, jnp.max(tot - gb + xi, axis=0, keepdims=True),
             jnp.zeros((4, LANES), F32)], axis=0)
        return carry

    lax.fori_loop(0, n_chunks, loop_a, 0)

    def loop_m(i, m):
        mf, mb = m
        sf = stat_ref[i]
        nf = jnp.maximum(sf[0:1] + mf, sf[1:2])
        m_ref[0, i] = jnp.concatenate([mf, nf, jnp.zeros((6, LANES), F32)], axis=0)
        cb = n_chunks - 1 - i
        sb = stat_ref[cb]
        nb = jnp.maximum(sb[2:3] + mb, sb[3:4])
        m_ref[1, cb] = jnp.concatenate([mb, nb, jnp.zeros((6, LANES), F32)], axis=0)
        return nf, nb

    zero_row = jnp.zeros((1, LANES), F32)
    lax.fori_loop(0, n_chunks, loop_m, (zero_row, zero_row))

    lane = lax.broadcasted_iota(jnp.int32, (CHUNK, LANES), 1)
    lane1 = lax.broadcasted_iota(jnp.int32, (1, LANES), 1)
    vrow = lax.broadcasted_iota(jnp.int32, (CHUNK, hd), 0)

    def pick(x, lane_ids, f_lane):
        return jnp.sum(jnp.where(lane_ids == f_lane, x, 0.0), axis=1, keepdims=True)

    def loop_p(c, carry):
        rows = _chunk_rows(c)
        q = qc_ref[rows, :]
        k = kc_ref[rows, :]
        v = jnp.where(vrow + c * CHUNK < PAD, 0.0, v_ref[0, rows, :].astype(F32))
        v1 = jnp.concatenate([v, jnp.ones((CHUNK, hd), F32)], axis=1)
        v1b = v1.astype(BF16)
        qk = _nt(q, k)
        lf = lf_ref[rows, :]
        gf = gf_ref[rows, :]
        xi = pltpu.roll(x_ref[rows, :], ML_HEADS, axis=1)
        for d in range(2):
            reverse = d == 1
            f_lane = 2 * ML_HEADS * d + ML_HEADS + head
            g_all = (gf[CHUNK - 1:CHUNK] - gf + lf) if reverse else gf
            g_col = pick(g_all, lane, f_lane)
            w_all = jnp.where(lane == f_lane, xi - g_all, 0.0)
            w_col = jnp.sum(w_all, axis=1, keepdims=True)
            sel = _selector(jnp.full((8, 2 * LANES), f_lane, jnp.int32), 3)
            w_row = _nt(sel, _pack3(w_all))
            dmat = g_col + jnp.concatenate([w_row] * (CHUNK // 8), axis=0)
            dmat = jnp.where(_scan_tri((CHUNK, CHUNK), reverse), dmat, -jnp.inf)
            mm = m_ref[d, c]
            m_st = pick(mm[0:1], lane1, f_lane)
            m_new = pick(mm[1:2], lane1, f_lane)
            g_last = g_col[0:1] if reverse else g_col[CHUNK - 1:CHUNK]
            m_inter = g_col + m_st
            m_t = jnp.maximum(m_inter, jnp.max(dmat, axis=1, keepdims=True))
            sw = jnp.exp(dmat - m_t) * qk
            a_ref[d, rows, :] = jnp.dot(sw.astype(BF16), v1b, preferred_element_type=F32)
            wi_ref[d, rows, :] = jnp.broadcast_to(jnp.exp(m_inter - m_t), (CHUNK, LANES))
            em_ref[d, rows, :] = jnp.broadcast_to(jnp.exp(-m_t), (CHUNK, LANES))
            w_end = jnp.exp(g_last + w_col - m_new)
            wv_ref[d, rows, :] = (w_end * v1).astype(BF16)
            wo_ref[d, c] = jnp.broadcast_to(jnp.exp(g_last + m_st - m_new), (8, LANES))
        return carry

    lax.fori_loop(0, n_chunks, loop_p, 0, unroll=3)

    def state_step(c, d):
        rows = _chunk_rows(c)
        st = st_ref[d]
        bm = jnp.dot(qc_ref[rows, :], st.astype(BF16), preferred_element_type=F32)
        wi = wi_ref[d, rows, :]
        tot = a_ref[d, rows, :] + jnp.concatenate([wi, wi], axis=1) * bm
        y_ref[rows, :] += tot[:, :hd] / jnp.maximum(jnp.abs(tot[:, hd:]), em_ref[d, rows, :])
        st_ref[d] = wo_ref[d, c][0:1, 0:1] * st + _tn(kc_ref[rows, :], wv_ref[d, rows, :])

    def loop_s(i, carry):
        state_step(i, 0)
        state_step(n_chunks - 1 - i, 1)
        return carry

    lax.fori_loop(0, n_chunks, loop_s, 0)

    y = y_ref[...]
    y = y * lax.rsqrt(jnp.mean(y * y, axis=1, keepdims=True) + RMS_EPS) * nw_ref[...]
    y = y * _sigmoid(og_ref[0].astype(F32))
    o_ref[0] = jnp.where(row < PAD, 0.0, y).astype(BF16)


def _mlstm(proj, gates, conv_w, conv_b, gate_bias, norm_w):
    bsz, tp, _ = proj.shape
    hd = ML_HEAD_DIM
    nc = tp // CHUNK
    base = 3 * D_A // hd

    def col_block(j0):
        return pl.BlockSpec((1, tp, hd), lambda b, h: (b, 0, j0 + h))

    def par_block(nrow, j0):
        return pl.BlockSpec((nrow, hd), lambda b, h: (0, j0 + h))

    gb = jnp.pad(gate_bias.astype(F32), (0, LANES - gate_bias.shape[0])).reshape(1, LANES)
    return pl.pallas_call(
        _mlstm_kernel,
        grid=(bsz, ML_HEADS),
        in_specs=[col_block(base), col_block(base + ML_HEADS), col_block(base + 2 * ML_HEADS),
                  col_block(base + 3 * ML_HEADS),
                  pl.BlockSpec((1, tp, LANES), lambda b, h: (b, 0, 0)),
                  par_block(CONV_W, 0), par_block(CONV_W, ML_HEADS),
                  par_block(1, 0), par_block(1, ML_HEADS),
                  pl.BlockSpec((1, LANES), lambda b, h: (0, 0)),
                  par_block(1, 0)],
        out_specs=pl.BlockSpec((1, tp, hd), lambda b, h: (b, 0, h)),
        out_shape=jax.ShapeDtypeStruct((bsz, tp, D_B), BF16),
        scratch_shapes=[pltpu.VMEM((tp + 16, 2 * hd), F32),
                        pltpu.VMEM((tp, hd), BF16),
                        pltpu.VMEM((tp, hd), BF16),
                        pltpu.VMEM((tp, LANES), F32),
                        pltpu.VMEM((tp, LANES), F32),
                        pltpu.VMEM((tp, LANES), F32),
                        pltpu.VMEM((nc, 8, LANES), F32),
                        pltpu.VMEM((2, nc, 8, LANES), F32),
                        pltpu.VMEM((2, tp, 2 * hd), F32),
                        pltpu.VMEM((2, tp, LANES), F32),
                        pltpu.VMEM((2, tp, LANES), F32),
                        pltpu.VMEM((2, tp, 2 * hd), BF16),
                        pltpu.VMEM((2, nc, 8, LANES), F32),
                        pltpu.VMEM((tp, hd), F32),
                        pltpu.VMEM((2, hd, 2 * hd), F32)],
        compiler_params=_cparams("parallel", "arbitrary"),
    )(proj, proj, proj, proj, gates, conv_w.astype(F32), conv_w.astype(F32),
      conv_b.astype(F32).reshape(1, -1), conv_b.astype(F32).reshape(1, -1), gb,
      norm_w.astype(F32).reshape(1, -1))


def _ssd_kernel(z_ref, x_ref, b_ref, c_ref, dt_ref, cwx_ref, cwb_ref, cwc_ref, cbx_ref, cbb_ref,
                cbc_ref, dtb_ref, alog_ref, dsk_ref, nw_ref, o_ref,
                src_ref, xs_ref, bc_ref, cc_ref, dts_ref, da_ref, cum_ref, sel_ref, rsel_ref,
                ec_ref, xw_ref, ecl_ref, y_ref, st_ref):
    tp = x_ref.shape[1]
    n_chunks = tp // CHUNK
    grp = pl.program_id(1)
    gw = SSD_GW
    ns = SSD_STATE
    hdim = SSD_HEAD_DIM

    _fill_conv_src(src_ref, (x_ref, b_ref, c_ref), tp)
    row = lax.broadcasted_iota(jnp.int32, (tp, LANES), 0)
    lane_all = lax.broadcasted_iota(jnp.int32, (tp, LANES), 1)
    dt_all = jnp.where((row < PAD) | (lane_all >= 2 * SSD_HEADS), 0.0,
                       _softplus(dt_ref[0] + dtb_ref[...]))
    dts_ref[...] = dt_all
    da_ref[...] = dt_all * (-jnp.exp(alog_ref[...]))
    st_ref[...] = jnp.zeros_like(st_ref)

    col_head = lax.broadcasted_iota(jnp.int32, (gw, 2 * LANES), 0) // hdim
    row_head = lax.broadcasted_iota(jnp.int32, (8 * SSD_HPG, 2 * LANES), 0) // 8
    for d in range(2):
        lane0 = SSD_HEADS * d + SSD_HPG * grp
        sel_ref[d, 0] = _selector(lane0 + col_head, 3)
        sel_ref[d, 1] = _selector(lane0 + col_head, 2)
        rsel_ref[d] = _selector(lane0 + row_head, 3)

    conv_w = jnp.concatenate([cwx_ref[...], cwb_ref[...], cwc_ref[...]], axis=1)
    conv_b = jnp.concatenate([cbx_ref[...], cbb_ref[...], cbc_ref[...]], axis=1)

    def loop_a(c, carry):
        rows = _chunk_rows(c)
        a = _conv_silu(src_ref, conv_w, conv_b, c)
        xs_ref[rows, :] = a[:, :gw]
        bc_ref[rows, :] = a[:, gw:gw + ns].astype(BF16)
        cc_ref[rows, :] = a[:, gw + ns:].astype(BF16)
        cum_ref[rows, :] = _cumsum_rows(da_ref[rows, :])
        return carry

    lax.fori_loop(0, n_chunks, loop_a, 0)

    blk4 = lax.broadcasted_iota(jnp.int32, (CHUNK, gw), 1) // hdim

    def loop_p(c, carry):
        rows = _chunk_rows(c)
        cumf = cum_ref[rows, :]
        cumb = cumf[CHUNK - 1:CHUNK] - cumf + da_ref[rows, :]
        pk_dt = _pack2(dts_ref[rows, :])
        xs = xs_ref[rows, :]
        bm = bc_ref[rows, :]
        cb4 = _nt(cc_ref[rows, :], jnp.concatenate([bm] * SSD_HPG, axis=0))
        y = dsk_ref[...] * xs
        for d in range(2):
            reverse = d == 1
            pk = _pack3(cumb if reverse else cumf)
            cum4 = _nt(pk, sel_ref[d, 0])
            dt4 = _nt(pk_dt, sel_ref[d, 1])
            row32 = _nt(rsel_ref[d], pk)
            row_cum = jnp.concatenate([row32[8 * r:8 * r + 8] for r in range(SSD_HPG)], axis=1)
            seg4 = cum4 - jnp.concatenate([row_cum] * (CHUNK // 8), axis=0)
            decay4 = jnp.where(_scan_tri((CHUNK, gw), reverse), jnp.exp(seg4), 0.0)
            xdt = xs * dt4
            xdt_b = xdt.astype(BF16)
            xdiag = jnp.concatenate([jnp.where(blk4 == r, xdt_b, jnp.zeros_like(xdt_b))
                                     for r in range(SSD_HPG)], axis=0)
            y = y + jnp.dot((decay4 * cb4).astype(BF16), xdiag, preferred_element_type=F32)
            cum_last = cum4[0:1] if reverse else cum4[CHUNK - 1:CHUNK]
            ec_ref[d, rows, :] = jnp.exp(cum4)
            xw_ref[d, rows, :] = (jnp.exp(cum_last - cum4) * xdt).astype(BF16)
            ecl_ref[d, c] = jnp.broadcast_to(jnp.exp(cum_last), (8, gw))
        y_ref[rows, :] = y
        return carry

    lax.fori_loop(0, n_chunks, loop_p, 0, unroll=3)

    def state_step(c, d):
        rows = _chunk_rows(c)
        st = st_ref[d]
        y_ref[rows, :] += ec_ref[d, rows, :] * jnp.dot(cc_ref[rows, :], st.astype(BF16),
                                                       preferred_element_type=F32)
        st_ref[d] = ecl_ref[d, c][0:1] * st + _tn(bc_ref[rows, :], xw_ref[d, rows, :])

    def loop_s(i, carry):
        state_step(i, 0)
        state_step(n_chunks - 1 - i, 1)
        return carry

    lax.fori_loop(0, n_chunks, loop_s, 0)

    rowg = lax.broadcasted_iota(jnp.int32, (tp, gw), 0)
    y = y_ref[...] * _silu(z_ref[0].astype(F32))
    y = y * lax.rsqrt(jnp.mean(y * y, axis=1, keepdims=True) + RMS_EPS) * nw_ref[...]
    o_ref[0] = jnp.where(rowg < PAD, 0.0, y).astype(BF16)


def _ssd(proj, dt_raw, conv_w, conv_b, dt_bias, a_log, d_skip, norm_w):
    bsz, tp, _ = proj.shape
    gw, ns = SSD_GW, SSD_STATE
    nc = tp // CHUNK
    xb, bb, cb = D_INNER // gw, 2 * D_INNER // ns, 2 * D_INNER // ns + SSD_GROUPS

    def col_block(width, j0):
        return pl.BlockSpec((1, tp, width), lambda b, g: (b, 0, j0 + g))

    def par_block(nrow, width, j0):
        return pl.BlockSpec((nrow, width), lambda b, g: (0, j0 + g))

    def lane_row(v):
        v = v.astype(F32).reshape(-1)
        return jnp.pad(v, (0, LANES - v.shape[0])).reshape(1, LANES)

    cw = conv_w.astype(F32)
    cbias = conv_b.astype(F32).reshape(1, -1)
    xb_c, bb_c, cb_c = 0, D_INNER // ns, D_INNER // ns + SSD_GROUPS
    dsk = jnp.repeat(d_skip.astype(F32), SSD_HEAD_DIM).reshape(1, D_INNER)
    full_row = pl.BlockSpec((1, LANES), lambda b, g: (0, 0))
    return pl.pallas_call(
        _ssd_kernel,
        grid=(bsz, SSD_GROUPS),
        in_specs=[col_block(gw, 0), col_block(gw, xb), col_block(ns, bb), col_block(ns, cb),
                  pl.BlockSpec((1, tp, LANES), lambda b, g: (b, 0, 0)),
                  par_block(CONV_W, gw, xb_c), par_block(CONV_W, ns, bb_c), par_block(CONV_W, ns, cb_c),
                  par_block(1, gw, xb_c), par_block(1, ns, bb_c), par_block(1, ns, cb_c),
                  full_row, full_row, par_block(1, gw, 0), par_block(1, gw, 0)],
        out_specs=pl.BlockSpec((1, tp, gw), lambda b, g: (b, 0, g)),
        out_shape=jax.ShapeDtypeStruct((bsz, tp, D_INNER), BF16),
        scratch_shapes=[pltpu.VMEM((tp + 16, gw + 2 * ns), F32),
                        pltpu.VMEM((tp, gw), F32),
                        pltpu.VMEM((tp, ns), BF16),
                        pltpu.VMEM((tp, ns), BF16),
                        pltpu.VMEM((tp, LANES), F32),
                        pltpu.VMEM((tp, LANES), F32),
                        pltpu.VMEM((tp, LANES), F32),
                        pltpu.VMEM((2, 2, gw, 2 * LANES), BF16),
                        pltpu.VMEM((2, 8 * SSD_HPG, 2 * LANES), BF16),
                        pltpu.VMEM((2, tp, gw), F32),
                        pltpu.VMEM((2, tp, gw), BF16),
                        pltpu.VMEM((2, nc, 8, gw), F32),
                        pltpu.VMEM((tp, gw), F32),
                        pltpu.VMEM((2, ns, gw), F32)],
        compiler_params=_cparams("parallel", "arbitrary"),
    )(proj, proj, proj, proj, dt_raw, cw, cw, cw, cbias, cbias, cbias,
      lane_row(dt_bias), lane_row(a_log), dsk, norm_w.astype(F32).reshape(1, -1))


def _final_kernel(x_ref, nw_ref, o_ref):
    seq = o_ref.shape[1]
    step = 256
    for r0 in range(0, seq, step):
        x = x_ref[0, PAD + N_META + r0:PAD + N_META + r0 + step, :]
        o_ref[0, r0:r0 + step, :] = _rms_rows(x, nw_ref[...])


def _final_norm(h3, norm_w, seq):
    bsz, tp, _ = h3.shape
    return pl.pallas_call(
        _final_kernel,
        grid=(bsz,),
        in_specs=[pl.BlockSpec((1, tp, D_MODEL), lambda b: (b, 0, 0)), _resident((1, D_MODEL))],
        out_specs=pl.BlockSpec((1, seq, D_MODEL), lambda b: (b, 0, 0)),
        out_shape=jax.ShapeDtypeStruct((bsz, seq, D_MODEL), F32),
        compiler_params=_cparams("parallel"),
    )(h3, norm_w.reshape(1, D_MODEL))


def _even_mixer(h2, bsz, tp, norm_w, w_in, rpb, conv_w, conv_b, gate_bias, ml_norm_w, w_out):
    proj, gates = _norm_proj(h2, norm_w, w_in, EV_MAIN)
    proj = proj.reshape(bsz, tp, EV_MAIN)
    y_a = _natten(proj, _na_bias_table(rpb))
    y_b = _mlstm(proj, gates.reshape(bsz, tp, LANES), conv_w, conv_b, gate_bias, ml_norm_w)
    return _out_proj([y_a.reshape(bsz * tp, D_A), y_b.reshape(bsz * tp, D_B)],
                     [w_out[:D_A], w_out[D_A:]], h2)


def _odd_mixer(h2, bsz, tp, norm_w, w_in, conv_w, conv_b, dt_bias, a_log, d_skip, ssd_norm_w, w_out):
    proj, dt_raw = _norm_proj(h2, norm_w, w_in, OD_MAIN)
    y = _ssd(proj.reshape(bsz, tp, OD_MAIN), dt_raw.reshape(bsz, tp, LANES), conv_w, conv_b,
             dt_bias, a_log, d_skip, ssd_norm_w)
    return _out_proj([y.reshape(bsz * tp, D_INNER)], [w_out], h2)


def kernel(x, meta_tokens, ffn1_norm, ffn1_gate, ffn1_up, ffn1_down, mix_norm, ffn2_norm, ffn2_gate, ffn2_up, ffn2_down, ev_w_in, na_rpb, ml_conv_w, ml_conv_b, ml_gate_bias, ml_norm_w, ev_w_out, od_w_in, ssd_conv_w, ssd_conv_b, ssd_dt_bias, ssd_a_log, ssd_d, ssd_norm_w, od_w_out, final_norm):
    bsz, seq, _ = x.shape
    tp = PAD + N_META + seq
    assert x.shape[2] == D_MODEL and seq % GRID_W == 0 and tp % ROW_TILE == 0
    assert seq // GRID_W >= NA_WIN_H
    head = jnp.concatenate([jnp.zeros((PAD, D_MODEL), F32), meta_tokens.astype(F32)], axis=0)
    h = jnp.concatenate([jnp.broadcast_to(head[None], (bsz, PAD + N_META, D_MODEL)), x], axis=1)
    h2 = h.reshape(bsz * tp, D_MODEL)
    for layer in range(ffn1_norm.shape[0]):
        h2 = _ffn(h2, ffn1_norm[layer], ffn1_gate[layer], ffn1_up[layer], ffn1_down[layer])
        i = layer // 2
        if layer % 2 == 0:
            h2 = _even_mixer(h2, bsz, tp, mix_norm[layer], ev_w_in[i], na_rpb[i], ml_conv_w[i],
                             ml_conv_b[i], ml_gate_bias[i], ml_norm_w[i], ev_w_out[i])
        else:
            h2 = _odd_mixer(h2, bsz, tp, mix_norm[layer], od_w_in[i], ssd_conv_w[i], ssd_conv_b[i],
                            ssd_dt_bias[i], ssd_a_log[i], ssd_d[i], ssd_norm_w[i], od_w_out[i])
        h2 = _ffn(h2, ffn2_norm[layer], ffn2_gate[layer], ffn2_up[layer], ffn2_down[layer])
    return _final_norm(h2.reshape(bsz, tp, D_MODEL), final_norm, seq)
```

```python
import functools

import jax
import jax.numpy as jnp
import numpy as np
from jax import lax
from jax.experimental import pallas as pl
from jax.experimental.pallas import tpu as pltpu

F32 = jnp.float32
BF16 = jnp.bfloat16

D_MODEL = 1024
N_META = 16
GRID_W = 64
CHUNK = 64
PAD = CHUNK - N_META
CONV_W = 5
CONV_HALF = (CONV_W - 1) // 2
D_FF = 2816
RMS_EPS = 1e-6

NA_HEADS = 8
NA_HEAD_DIM = 64
NA_WIN_H = 8
NA_WIN_W = 16
ML_HEADS = 4
ML_HEAD_DIM = 128
D_A = NA_HEADS * NA_HEAD_DIM
D_B = ML_HEADS * ML_HEAD_DIM
EV_MAIN = 3 * D_A + 4 * D_B

SSD_HEAD_DIM = 64
SSD_HEADS = 32
SSD_GROUPS = 8
SSD_HPG = SSD_HEADS // SSD_GROUPS
SSD_STATE = 128
D_INNER = SSD_HEADS * SSD_HEAD_DIM
SSD_GW = SSD_HPG * SSD_HEAD_DIM
OD_MAIN = 2 * D_INNER + 2 * SSD_GROUPS * SSD_STATE

LANES = 128
HALF = LANES // 2
FF_CHUNK = 256
ROW_TILE = 528
FFN_ROW_TILE = 1056
SCAN_UNROLL = 3
VMEM_LIMIT = 56 * 1024 * 1024


def _cparams(*sem):
    return pltpu.CompilerParams(dimension_semantics=sem, vmem_limit_bytes=VMEM_LIMIT)


def _resident(shape):
    nd = len(shape)
    return pl.BlockSpec(shape, lambda *_: (0,) * nd, pipeline_mode=pl.Buffered(1))


def _rms_rows(x, w_row):
    ms = jnp.mean(x * x, axis=-1, keepdims=True)
    return x * lax.rsqrt(ms + RMS_EPS) * w_row


def _sigmoid(x):
    return 1.0 / (1.0 + jnp.exp(-x))


def _silu(x):
    return x * _sigmoid(x)


def _softplus(x):
    return jnp.maximum(x, 0.0) + jnp.log(1.0 + jnp.exp(-jnp.abs(x)))


def _log_sigmoid(x):
    return -_softplus(-x)


def _nt(a, b):
    return lax.dot_general(a, b, (((1,), (1,)), ((), ())), preferred_element_type=F32)


def _tn(a, b):
    return lax.dot_general(a, b, (((0,), (0,)), ((), ())), preferred_element_type=F32)


def _ffn_kernel(x_ref, nw_ref, wgu_ref, wd_ref, o_ref, xn_ref, acc_ref):
    xn_ref[...] = _rms_rows(x_ref[...], nw_ref[...]).astype(BF16)
    acc_ref[...] = jnp.zeros_like(acc_ref)

    def body(f, carry):
        gu = jnp.dot(xn_ref[...], wgu_ref[f], preferred_element_type=F32)
        act = (_silu(gu[:, :FF_CHUNK]) * gu[:, FF_CHUNK:]).astype(BF16)
        acc_ref[...] += jnp.dot(act, wd_ref[f], preferred_element_type=F32)
        return carry

    lax.fori_loop(0, wgu_ref.shape[0], body, 0)
    o_ref[...] = x_ref[...] + 0.5 * acc_ref[...]


def _ffn(h2, norm_w, w_gate, w_up, w_down):
    rows = h2.shape[0]
    nf = D_FF // FF_CHUNK
    wg = w_gate.astype(BF16).reshape(D_MODEL, nf, FF_CHUNK)
    wu = w_up.astype(BF16).reshape(D_MODEL, nf, FF_CHUNK)
    wgu = jnp.concatenate([wg, wu], axis=-1).transpose(1, 0, 2)
    wd = w_down.astype(BF16).reshape(nf, FF_CHUNK, D_MODEL)
    row_spec = pl.BlockSpec((FFN_ROW_TILE, D_MODEL), lambda i: (i, 0))
    return pl.pallas_call(
        _ffn_kernel,
        grid=(rows // FFN_ROW_TILE,),
        in_specs=[row_spec, _resident((1, D_MODEL)), _resident(wgu.shape), _resident(wd.shape)],
        out_specs=row_spec,
        out_shape=jax.ShapeDtypeStruct(h2.shape, F32),
        scratch_shapes=[pltpu.VMEM((FFN_ROW_TILE, D_MODEL), BF16), pltpu.VMEM((FFN_ROW_TILE, D_MODEL), F32)],
        compiler_params=_cparams("parallel"),
    )(h2, norm_w.reshape(1, D_MODEL), wgu, wd)


def _norm_proj_kernel(x_ref, nw_ref, w_ref, ws_ref, o_ref, os_ref, xn_ref, *, col_chunk):
    xn_ref[...] = _rms_rows(x_ref[...], nw_ref[...]).astype(BF16)
    for c in range(w_ref.shape[1] // col_chunk):
        cols = slice(c * col_chunk, (c + 1) * col_chunk)
        o_ref[:, cols] = jnp.dot(xn_ref[...], w_ref[:, cols], preferred_element_type=F32).astype(BF16)
    os_ref[...] = jnp.dot(xn_ref[...], ws_ref[...], preferred_element_type=F32)


def _norm_proj(h2, norm_w, w_in, n_main):
    rows = h2.shape[0]
    w_main = w_in[:, :n_main].astype(BF16)
    n_small = w_in.shape[1] - n_main
    w_small = jnp.pad(w_in[:, n_main:], ((0, 0), (0, LANES - n_small))).astype(BF16)
    row_spec = pl.BlockSpec((ROW_TILE, D_MODEL), lambda i: (i, 0))
    return pl.pallas_call(
        functools.partial(_norm_proj_kernel, col_chunk=2 * FF_CHUNK),
        grid=(rows // ROW_TILE,),
        in_specs=[row_spec, _resident((1, D_MODEL)), _resident(w_main.shape), _resident(w_small.shape)],
        out_specs=[pl.BlockSpec((ROW_TILE, n_main), lambda i: (i, 0)),
                   pl.BlockSpec((ROW_TILE, LANES), lambda i: (i, 0))],
        out_shape=[jax.ShapeDtypeStruct((rows, n_main), BF16),
                   jax.ShapeDtypeStruct((rows, LANES), F32)],
        scratch_shapes=[pltpu.VMEM((ROW_TILE, D_MODEL), BF16)],
        compiler_params=_cparams("parallel"),
    )(h2, norm_w.reshape(1, D_MODEL), w_main, w_small)


def _out_proj_kernel(*refs):
    n = (len(refs) - 2) // 2
    h_ref, o_ref = refs[2 * n], refs[2 * n + 1]
    acc = h_ref[...]
    for y_ref, w_ref in zip(refs[:n], refs[n:2 * n]):
        acc = acc + jnp.dot(y_ref[...], w_ref[...], preferred_element_type=F32)
    o_ref[...] = acc


def _out_proj(ys, ws, h2):
    rows = h2.shape[0]
    row_spec = pl.BlockSpec((ROW_TILE, D_MODEL), lambda i: (i, 0))
    ws = [w.astype(BF16) for w in ws]
    return pl.pallas_call(
        _out_proj_kernel,
        grid=(rows // ROW_TILE,),
        in_specs=([pl.BlockSpec((ROW_TILE, y.shape[1]), lambda i: (i, 0)) for y in ys]
                  + [_resident(w.shape) for w in ws] + [row_spec]),
        out_specs=row_spec,
        out_shape=jax.ShapeDtypeStruct(h2.shape, F32),
        compiler_params=_cparams("parallel"),
    )(*ys, *ws, h2)


def _na_bias_table(rpb):
    qc = np.arange(GRID_W)[:, None]
    kc = np.arange(GRID_W)[None, :]
    dc = np.clip(kc - qc, 1 - NA_WIN_W, NA_WIN_W - 1) + (NA_WIN_W - 1)
    win_c0 = np.clip(qc - NA_WIN_W // 2, 0, GRID_W - NA_WIN_W)
    col_ok = (kc >= win_c0) & (kc < win_c0 + NA_WIN_W)
    onehot = (dc[:, :, None] == np.arange(2 * NA_WIN_W - 1)).astype(np.float32)
    toep = jnp.einsum('hdc,qkc->hdqk', rpb.astype(F32), onehot, precision=lax.Precision.HIGHEST)
    toep = jnp.where(col_ok[None, None], toep, -jnp.inf)
    tbl = jnp.stack([toep[:, d0:d0 + NA_WIN_H] for d0 in range(NA_WIN_H)], axis=1)
    tbl = tbl.transpose(0, 1, 3, 2, 4).reshape(NA_HEADS, NA_WIN_H, GRID_W, NA_WIN_H * GRID_W)
    meta = jnp.zeros((NA_HEADS, NA_WIN_H, GRID_W, N_META), F32)
    return jnp.concatenate([tbl, meta], axis=-1)


def _natten_kernel(q_ref, k_ref, v_ref, bias_ref, o_ref, *, n_rows):
    scale = NA_HEAD_DIM ** -0.5
    g0 = PAD + N_META
    n_keys = NA_WIN_H * GRID_W

    cols = [slice(h * NA_HEAD_DIM, (h + 1) * NA_HEAD_DIM) for h in range(NA_HEADS)]

    def attend(q, k, v, bias):
        s = [_nt(q[:, c], k[:, c]) for c in cols]
        p, l = [], []
        for h in range(NA_HEADS):
            sh = s[h] * scale if bias is None else s[h] * scale + bias(h)
            ph = jnp.exp(sh - jnp.max(sh, axis=1, keepdims=True))
            l.append(jnp.sum(ph, axis=1, keepdims=True))
            p.append(ph.astype(BF16))
        o = [jnp.dot(p[h], v[:, cols[h]], preferred_element_type=F32) for h in range(NA_HEADS)]
        return jnp.concatenate([o[h] / l[h] for h in range(NA_HEADS)], axis=1).astype(BF16)

    k_meta = k_ref[0, PAD:g0, :]
    v_meta = v_ref[0, PAD:g0, :]
    o_ref[0, :PAD, :] = jnp.zeros((PAD, D_A), BF16)
    o_ref[0, PAD:g0, :] = attend(q_ref[0, PAD:g0, :], k_meta, v_meta, None)

    def row_body(r, carry):
        first = jnp.clip(r - NA_WIN_H // 2, 0, n_rows - NA_WIN_H)
        d0 = first - r + (NA_WIN_H - 1)
        q_rows = pl.ds(pl.multiple_of(g0 + r * GRID_W, GRID_W), GRID_W)
        k_rows = pl.ds(pl.multiple_of(g0 + first * GRID_W, GRID_W), n_keys)
        k = jnp.concatenate([k_ref[0, k_rows, :], k_meta], axis=0)
        v = jnp.concatenate([v_ref[0, k_rows, :], v_meta], axis=0)
        o_ref[0, q_rows, :] = attend(q_ref[0, q_rows, :], k, v, lambda h: bias_ref[h, d0])
        return carry

    lax.fori_loop(0, n_rows, row_body, 0)


def _natten(proj, bias_tbl):
    bsz, tp, _ = proj.shape
    n_rows = (tp - PAD - N_META) // GRID_W

    def col_block(j):
        return pl.BlockSpec((1, tp, D_A), lambda b: (b, 0, j))

    return pl.pallas_call(
        functools.partial(_natten_kernel, n_rows=n_rows),
        grid=(bsz,),
        in_specs=[col_block(0), col_block(1), col_block(2), _resident(bias_tbl.shape)],
        out_specs=pl.BlockSpec((1, tp, D_A), lambda b: (b, 0, 0)),
        out_shape=jax.ShapeDtypeStruct((bsz, tp, D_A), BF16),
        compiler_params=_cparams("parallel"),
    )(proj, proj, proj, bias_tbl)


def _fill_conv_src(src_ref, parts, tp):
    width = src_ref.shape[1]
    src_ref[0:8 + PAD, :] = jnp.zeros((8 + PAD, width), F32)
    src_ref[8 + tp:16 + tp, :] = jnp.zeros((8, width), F32)
    off = 0
    for p in parts:
        w = p.shape[2]
        src_ref[8 + PAD:8 + tp, off:off + w] = p[0, PAD:tp, :].astype(F32)
        off += w


def _conv_silu(src_ref, w, b, c):
    x = src_ref[pl.ds(pl.multiple_of(c * CHUNK, CHUNK), CHUNK + 16), :]
    acc = b + w[0:1, :] * x[8 - CONV_HALF:8 - CONV_HALF + CHUNK]
    for j in range(1, CONV_W):
        lo = 8 - CONV_HALF + j
        acc = acc + w[j:j + 1, :] * x[lo:lo + CHUNK]
    row = lax.broadcasted_iota(jnp.int32, acc.shape, 0)
    return jnp.where(row + c * CHUNK < PAD, 0.0, _silu(acc))


def _chunk_rows(c):
    return pl.ds(pl.multiple_of(c * CHUNK, CHUNK), CHUNK)


def _scan_tri(shape, reverse):
    t = lax.broadcasted_iota(jnp.int32, shape, 0)
    s = lax.broadcasted_iota(jnp.int32, shape, 1) % CHUNK
    return (s >= t) if reverse else (s <= t)


def _cumsum_rows(x):
    row = lax.broadcasted_iota(jnp.int32, x.shape, 0)
    sh = 1
    while sh < CHUNK:
        x = x + jnp.where(row >= sh, pltpu.roll(x, sh, axis=0), 0.0)
        sh *= 2
    return x


def _pack3(x):
    hi = x.astype(BF16).astype(F32)
    r1 = x - hi
    mid = r1.astype(BF16).astype(F32)
    return jnp.concatenate([hi + pltpu.roll(mid, HALF, axis=1), r1 - mid], axis=1).astype(BF16)


def _pack2(x):
    hi = x.astype(BF16)
    return jnp.concatenate([hi, (x - hi.astype(F32)).astype(BF16)], axis=1)


def _selector(lane_of_row, terms):
    k = lax.broadcasted_iota(jnp.int32, lane_of_row.shape, 1)
    hit = (k == lane_of_row) | (k == lane_of_row + LANES)
    if terms == 3:
        hit = hit | (k == lane_of_row + HALF)
    return jnp.where(hit, 1.0, 0.0).astype(BF16)


def _mlstm_kernel(q_ref, k_ref, v_ref, og_ref, gates_ref, cwq_ref, cwk_ref, cbq_ref, cbk_ref,
                  gb_ref, nw_ref, o_ref, src_ref, qc_ref, kc_ref, x_ref, lf_ref, gf_ref,
                  stat_ref, m_ref, a_ref, wi_ref, em_ref, wv_ref, wo_ref, y_ref, st_ref):
    tp = q_ref.shape[1]
    n_chunks = tp // CHUNK
    head = pl.program_id(1)
    hd = ML_HEAD_DIM

    _fill_conv_src(src_ref, (q_ref, k_ref), tp)
    row = lax.broadcasted_iota(jnp.int32, (tp, LANES), 0)
    x_all = jnp.where(row < PAD, 0.0, gates_ref[0] + gb_ref[...])
    x_ref[...] = x_all
    lf_ref[...] = _log_sigmoid(x_all)
    st_ref[...] = jnp.zeros_like(st_ref)
    y_ref[...] = jnp.zeros_like(y_ref)

    conv_w = jnp.concatenate([cwq_ref[...], cwk_ref[...]], axis=1)
    conv_b = jnp.concatenate([cbq_ref[...], cbk_ref[...]], axis=1)

    def loop_a(c, carry):
        rows = _chunk_rows(c)
        qk = _conv_silu(src_ref, conv_w, conv_b, c)
        qc_ref[rows, :] = qk[:, :hd].astype(BF16)
        kc_ref[rows, :] = (qk[:, hd:] * (hd ** -0.5)).astype(BF16)
        lf = lf_ref[rows, :]
        gf = _cumsum_rows(lf)
        gf_ref[rows, :] = gf
        xi = pltpu.roll(x_ref[rows, :], ML_HEADS, axis=1)
        tot = gf[CHUNK - 1:CHUNK]
        gb = tot - gf + lf
        stat_ref[c] = jnp.concatenate(
            [tot, jnp.max(tot - gf + xi, axis=0, keepdims=True),
             tot, jnp.max(tot - gb + xi, axis=0, keepdims=True),
             jnp.zeros((4, LANES), F32)], axis=0)
        return carry

    lax.fori_loop(0, n_chunks, loop_a, 0)

    def loop_m(i, m):
        mf, mb = m
        sf = stat_ref[i]
        nf = jnp.maximum(sf[0:1] + mf, sf[1:2])
        m_ref[0, i] = jnp.concatenate([mf, nf, jnp.zeros((6, LANES), F32)], axis=0)
        cb = n_chunks - 1 - i
        sb = stat_ref[cb]
        nb = jnp.maximum(sb[2:3] + mb, sb[3:4])
        m_ref[1, cb] = jnp.concatenate([mb, nb, jnp.zeros((6, LANES), F32)], axis=0)
        return nf, nb

    zero_row = jnp.zeros((1, LANES), F32)
    lax.fori_loop(0, n_chunks, loop_m, (zero_row, zero_row))

    lane = lax.broadcasted_iota(jnp.int32, (CHUNK, LANES), 1)
    lane1 = lax.broadcasted_iota(jnp.int32, (1, LANES), 1)
    vrow = lax.broadcasted_iota(jnp.int32, (CHUNK, hd), 0)

    def pick(x, lane_ids, f_lane):
        return jnp.sum(jnp.where(lane_ids == f_lane, x, 0.0), axis=1, keepdims=True)

    units = [(u, d) for u in range(SCAN_UNROLL) for d in range(2)]
    f_lanes = [2 * ML_HEADS * d + ML_HEADS + head for d in range(2)]
    sels = [_selector(jnp.full((8, 2 * LANES), f, jnp.int32), 3) for f in f_lanes]

    def loop_p(i, carry):
        cs = [i * SCAN_UNROLL + u for u in range(SCAN_UNROLL)]
        rows = [_chunk_rows(c) for c in cs]
        qk = [_nt(qc_ref[r, :], kc_ref[r, :]) for r in rows]
        v1, g_col, w_col, w_pk = [], {}, {}, {}
        for u in range(SCAN_UNROLL):
            v = jnp.where(vrow + cs[u] * CHUNK < PAD, 0.0, v_ref[0, rows[u], :].astype(F32))
            v1.append(jnp.concatenate([v, jnp.ones((CHUNK, hd), F32)], axis=1))
            lf = lf_ref[rows[u], :]
            gf = gf_ref[rows[u], :]
            xi = pltpu.roll(x_ref[rows[u], :], ML_HEADS, axis=1)
            for d in range(2):
                g_all = (gf[CHUNK - 1:CHUNK] - gf + lf) if d == 1 else gf
                g_col[u, d] = pick(g_all, lane, f_lanes[d])
                w_all = jnp.where(lane == f_lanes[d], xi - g_all, 0.0)
                w_col[u, d] = jnp.sum(w_all, axis=1, keepdims=True)
                w_pk[u, d] = _pack3(w_all)
        w_row = {(u, d): _nt(sels[d], w_pk[u, d]) for u, d in units}
        sw = {}
        for u, d in units:
            reverse = d == 1
            gc = g_col[u, d]
            dmat = gc + jnp.concatenate([w_row[u, d]] * (CHUNK // 8), axis=0)
            dmat = jnp.where(_scan_tri((CHUNK, CHUNK), reverse), dmat, -jnp.inf)
            mm = m_ref[d, cs[u]]
            m_st = pick(mm[0:1], lane1, f_lanes[d])
            m_new = pick(mm[1:2], lane1, f_lanes[d])
            g_last = gc[0:1] if reverse else gc[CHUNK - 1:CHUNK]
            m_inter = gc + m_st
            m_t = jnp.maximum(m_inter, jnp.max(dmat, axis=1, keepdims=True))
            sw[u, d] = (jnp.exp(dmat - m_t) * qk[u]).astype(BF16)
            wi_ref[d, rows[u], :] = jnp.broadcast_to(jnp.exp(m_inter - m_t), (CHUNK, LANES))
            em_ref[d, rows[u], :] = jnp.broadcast_to(jnp.exp(-m_t), (CHUNK, LANES))
            w_end = jnp.exp(g_last + w_col[u, d] - m_new)
            wv_ref[d, rows[u], :] = (w_end * v1[u]).astype(BF16)
            wo_ref[d, cs[u]] = jnp.broadcast_to(jnp.exp(g_last + m_st - m_new), (8, LANES))
        v1b = [x.astype(BF16) for x in v1]
        intra = {(u, d): jnp.dot(sw[u, d], v1b[u], preferred_element_type=F32) for u, d in units}
        for u, d in units:
            a_ref[d, rows[u], :] = intra[u, d]
        return carry

    lax.fori_loop(0, n_chunks // SCAN_UNROLL, loop_p, 0)

    def loop_s(i, carry):
        cs = (i, n_chunks - 1 - i)
        rows = [_chunk_rows(c) for c in cs]
        st = [st_ref[d] for d in range(2)]
        inter = [jnp.dot(qc_ref[rows[d], :], st[d].astype(BF16), preferred_element_type=F32)
                 for d in range(2)]
        upd = [_tn(kc_ref[rows[d], :], wv_ref[d, rows[d], :]) for d in range(2)]
        for d in range(2):
            wi = wi_ref[d, rows[d], :]
            tot = a_ref[d, rows[d], :] + jnp.concatenate([wi, wi], axis=1) * inter[d]
            y_ref[rows[d], :] += tot[:, :hd] / jnp.maximum(jnp.abs(tot[:, hd:]), em_ref[d, rows[d], :])
            st_ref[d] = wo_ref[d, cs[d]][0:1, 0:1] * st[d] + upd[d]
        return carry

    lax.fori_loop(0, n_chunks, loop_s, 0)

    y = y_ref[...]
    y = y * lax.rsqrt(jnp.mean(y * y, axis=1, keepdims=True) + RMS_EPS) * nw_ref[...]
    y = y * _sigmoid(og_ref[0].astype(F32))
    o_ref[0] = jnp.where(row < PAD, 0.0, y).astype(BF16)


def _mlstm(proj, gates, conv_w, conv_b, gate_bias, norm_w):
    bsz, tp, _ = proj.shape
    hd = ML_HEAD_DIM
    nc = tp // CHUNK
    base = 3 * D_A // hd

    def col_block(j0):
        return pl.BlockSpec((1, tp, hd), lambda b, h: (b, 0, j0 + h))

    def par_block(nrow, j0):
        return pl.BlockSpec((nrow, hd), lambda b, h: (0, j0 + h))

    gb = jnp.pad(gate_bias.astype(F32), (0, LANES - gate_bias.shape[0])).reshape(1, LANES)
    return pl.pallas_call(
        _mlstm_kernel,
        grid=(bsz, ML_HEADS),
        in_specs=[col_block(base), col_block(base + ML_HEADS), col_block(base + 2 * ML_HEADS),
                  col_block(base + 3 * ML_HEADS),
                  pl.BlockSpec((1, tp, LANES), lambda b, h: (b, 0, 0)),
                  par_block(CONV_W, 0), par_block(CONV_W, ML_HEADS),
                  par_block(1, 0), par_block(1, ML_HEADS),
                  pl.BlockSpec((1, LANES), lambda b, h: (0, 0)),
                  par_block(1, 0)],
        out_specs=pl.BlockSpec((1, tp, hd), lambda b, h: (b, 0, h)),
        out_shape=jax.ShapeDtypeStruct((bsz, tp, D_B), BF16),
        scratch_shapes=[pltpu.VMEM((tp + 16, 2 * hd), F32),
                        pltpu.VMEM((tp, hd), BF16),
                        pltpu.VMEM((tp, hd), BF16),
                        pltpu.VMEM((tp, LANES), F32),
                        pltpu.VMEM((tp, LANES), F32),
                        pltpu.VMEM((tp, LANES), F32),
                        pltpu.VMEM((nc, 8, LANES), F32),
                        pltpu.VMEM((2, nc, 8, LANES), F32),
                        pltpu.VMEM((2, tp, 2 * hd), F32),
                        pltpu.VMEM((2, tp, LANES), F32),
                        pltpu.VMEM((2, tp, LANES), F32),
                        pltpu.VMEM((2, tp, 2 * hd), BF16),
                        pltpu.VMEM((2, nc, 8, LANES), F32),
                        pltpu.VMEM((tp, hd), F32),
                        pltpu.VMEM((2, hd, 2 * hd), F32)],
        compiler_params=_cparams("parallel", "arbitrary"),
    )(proj, proj, proj, proj, gates, conv_w.astype(F32), conv_w.astype(F32),
      conv_b.astype(F32).reshape(1, -1), conv_b.astype(F32).reshape(1, -1), gb,
      norm_w.astype(F32).reshape(1, -1))


def _ssd_kernel(z_ref, x_ref, b_ref, c_ref, dt_ref, cwx_ref, cwb_ref, cwc_ref, cbx_ref, cbb_ref,
                cbc_ref, dtb_ref, alog_ref, dsk_ref, nw_ref, o_ref,
                src_ref, xs_ref, bc_ref, cc_ref, dts_ref, da_ref, cum_ref, sel_ref, rsel_ref,
                pk_ref, cum4_ref, xdt_ref, xw_ref, ecl_ref, y_ref, st_ref):
    tp = x_ref.shape[1]
    n_chunks = tp // CHUNK
    assert n_chunks % SCAN_UNROLL == 0
    grp = pl.program_id(1)
    gw = SSD_GW
    ns = SSD_STATE
    hdim = SSD_HEAD_DIM

    _fill_conv_src(src_ref, (x_ref, b_ref, c_ref), tp)
    row = lax.broadcasted_iota(jnp.int32, (tp, LANES), 0)
    lane_all = lax.broadcasted_iota(jnp.int32, (tp, LANES), 1)
    dt_all = jnp.where((row < PAD) | (lane_all >= 2 * SSD_HEADS), 0.0,
                       _softplus(dt_ref[0] + dtb_ref[...]))
    dts_ref[...] = dt_all
    da_ref[...] = dt_all * (-jnp.exp(alog_ref[...]))
    st_ref[...] = jnp.zeros_like(st_ref)

    col_head = lax.broadcasted_iota(jnp.int32, (gw, 2 * LANES), 0) // hdim
    row_head = lax.broadcasted_iota(jnp.int32, (8 * SSD_HPG, 2 * LANES), 0) // 8
    for d in range(2):
        lane0 = SSD_HEADS * d + SSD_HPG * grp
        sel_ref[d, 0] = _selector(lane0 + col_head, 3)
        sel_ref[d, 1] = _selector(lane0 + col_head, 2)
        rsel_ref[d] = _selector(lane0 + row_head, 3)

    conv_w = jnp.concatenate([cwx_ref[...], cwb_ref[...], cwc_ref[...]], axis=1)
    conv_b = jnp.concatenate([cbx_ref[...], cbb_ref[...], cbc_ref[...]], axis=1)

    def loop_a(c, carry):
        rows = _chunk_rows(c)
        a = _conv_silu(src_ref, conv_w, conv_b, c)
        xs_ref[rows, :] = a[:, :gw]
        bc_ref[rows, :] = a[:, gw:gw + ns].astype(BF16)
        cc_ref[rows, :] = a[:, gw + ns:].astype(BF16)
        da = da_ref[rows, :]
        cumf = _cumsum_rows(da)
        cum_ref[0, rows, :] = cumf
        cum_ref[1, rows, :] = cumf[CHUNK - 1:CHUNK] - cumf + da
        return carry

    lax.fori_loop(0, n_chunks, loop_a, 0)

    eb_rows = tp // SCAN_UNROLL

    def loop_e(i, carry):
        rows = pl.ds(pl.multiple_of(i * eb_rows, CHUNK), eb_rows)
        pk = [_pack3(cum_ref[d, rows, :]) for d in range(2)]
        pk_dt = _pack2(dts_ref[rows, :])
        cum4 = [_nt(pk[d], sel_ref[d, 0]) for d in range(2)]
        dt4 = [_nt(pk_dt, sel_ref[d, 1]) for d in range(2)]
        xs = xs_ref[rows, :]
        for d in range(2):
            pk_ref[d, rows, :] = pk[d]
            cum4_ref[d, rows, :] = cum4[d]
            xdt_ref[d, rows, :] = xs * dt4[d]
        return carry

    lax.fori_loop(0, SCAN_UNROLL, loop_e, 0)

    blk4 = lax.broadcasted_iota(jnp.int32, (CHUNK, gw), 1) // hdim
    units = [(u, d) for u in range(SCAN_UNROLL) for d in range(2)]

    def loop_p(i, carry):
        cs = [i * SCAN_UNROLL + u for u in range(SCAN_UNROLL)]
        rows = [_chunk_rows(c) for c in cs]
        cb4 = [_nt(cc_ref[r, :], jnp.concatenate([bc_ref[r, :]] * SSD_HPG, axis=0)) for r in rows]
        row32 = {(u, d): _nt(rsel_ref[d], pk_ref[d, rows[u], :]) for u, d in units}
        m4, xdiag = {}, {}
        for u, d in units:
            reverse = d == 1
            cum4 = cum4_ref[d, rows[u], :]
            r32 = row32[u, d]
            row_cum = jnp.concatenate([r32[8 * r:8 * r + 8] for r in range(SSD_HPG)], axis=1)
            seg4 = cum4 - jnp.concatenate([row_cum] * (CHUNK // 8), axis=0)
            decay4 = jnp.where(_scan_tri((CHUNK, gw), reverse), jnp.exp(seg4), 0.0)
            m4[u, d] = (decay4 * cb4[u]).astype(BF16)
            xdt = xdt_ref[d, rows[u], :]
            xdt_b = xdt.astype(BF16)
            xdiag[u, d] = jnp.concatenate([jnp.where(blk4 == r, xdt_b, jnp.zeros_like(xdt_b))
                                           for r in range(SSD_HPG)], axis=0)
            cum_last = cum4[0:1] if reverse else cum4[CHUNK - 1:CHUNK]
            xw_ref[d, rows[u], :] = (jnp.exp(cum_last - cum4) * xdt).astype(BF16)
            ecl_ref[d, cs[u]] = jnp.broadcast_to(jnp.exp(cum_last), (8, gw))
        intra = {k: jnp.dot(m4[k], xdiag[k], preferred_element_type=F32) for k in units}
        for u in range(SCAN_UNROLL):
            y_ref[rows[u], :] = dsk_ref[...] * xs_ref[rows[u], :] + intra[u, 0] + intra[u, 1]
        return carry

    lax.fori_loop(0, n_chunks // SCAN_UNROLL, loop_p, 0)

    def loop_s(i, carry):
        cs = (i, n_chunks - 1 - i)
        rows = [_chunk_rows(c) for c in cs]
        st = [st_ref[d] for d in range(2)]
        inter = [jnp.dot(cc_ref[rows[d], :], st[d].astype(BF16), preferred_element_type=F32)
                 for d in range(2)]
        upd = [_tn(bc_ref[rows[d], :], xw_ref[d, rows[d], :]) for d in range(2)]
        for d in range(2):
            y_ref[rows[d], :] += jnp.exp(cum4_ref[d, rows[d], :]) * inter[d]
            st_ref[d] = ecl_ref[d, cs[d]][0:1] * st[d] + upd[d]
        return carry

    lax.fori_loop(0, n_chunks, loop_s, 0)

    rowg = lax.broadcasted_iota(jnp.int32, (tp, gw), 0)
    y = y_ref[...] * _silu(z_ref[0].astype(F32))
    y = y * lax.rsqrt(jnp.mean(y * y, axis=1, keepdims=True) + RMS_EPS) * nw_ref[...]
    o_ref[0] = jnp.where(rowg < PAD, 0.0, y).astype(BF16)


def _ssd(proj, dt_raw, conv_w, conv_b, dt_bias, a_log, d_skip, norm_w):
    bsz, tp, _ = proj.shape
    gw, ns = SSD_GW, SSD_STATE
    nc = tp // CHUNK
    xb, bb, cb = D_INNER // gw, 2 * D_INNER // ns, 2 * D_INNER // ns + SSD_GROUPS

    def col_block(width, j0):
        return pl.BlockSpec((1, tp, width), lambda b, g: (b, 0, j0 + g))

    def par_block(nrow, width, j0):
        return pl.BlockSpec((nrow, width), lambda b, g: (0, j0 + g))

    def lane_row(v):
        v = v.astype(F32).reshape(-1)
        return jnp.pad(v, (0, LANES - v.shape[0])).reshape(1, LANES)

    cw = conv_w.astype(F32)
    cbias = conv_b.astype(F32).reshape(1, -1)
    xb_c, bb_c, cb_c = 0, D_INNER // ns, D_INNER // ns + SSD_GROUPS
    dsk = jnp.repeat(d_skip.astype(F32), SSD_HEAD_DIM).reshape(1, D_INNER)
    full_row = pl.BlockSpec((1, LANES), lambda b, g: (0, 0))
    return pl.pallas_call(
        _ssd_kernel,
        grid=(bsz, SSD_GROUPS),
        in_specs=[col_block(gw, 0), col_block(gw, xb), col_block(ns, bb), col_block(ns, cb),
                  pl.BlockSpec((1, tp, LANES), lambda b, g: (b, 0, 0)),
                  par_block(CONV_W, gw, xb_c), par_block(CONV_W, ns, bb_c), par_block(CONV_W, ns, cb_c),
                  par_block(1, gw, xb_c), par_block(1, ns, bb_c), par_block(1, ns, cb_c),
                  full_row, full_row, par_block(1, gw, 0), par_block(1, gw, 0)],
        out_specs=pl.BlockSpec((1, tp, gw), lambda b, g: (b, 0, g)),
        out_shape=jax.ShapeDtypeStruct((bsz, tp, D_INNER), BF16),
        scratch_shapes=[pltpu.VMEM((tp + 16, gw + 2 * ns), F32),
                        pltpu.VMEM((tp, gw), F32),
                        pltpu.VMEM((tp, ns), BF16),
                        pltpu.VMEM((tp, ns), BF16),
                        pltpu.VMEM((tp, LANES), F32),
                        pltpu.VMEM((tp, LANES), F32),
                        pltpu.VMEM((2, tp, LANES), F32),
                        pltpu.VMEM((2, 2, gw, 2 * LANES), BF16),
                        pltpu.VMEM((2, 8 * SSD_HPG, 2 * LANES), BF16),
                        pltpu.VMEM((2, tp, 2 * LANES), BF16),
                        pltpu.VMEM((2, tp, gw), F32),
                        pltpu.VMEM((2, tp, gw), F32),
                        pltpu.VMEM((2, tp, gw), BF16),
                        pltpu.VMEM((2, nc, 8, gw), F32),
                        pltpu.VMEM((tp, gw), F32),
                        pltpu.VMEM((2, ns, gw), F32)],
        compiler_params=_cparams("parallel", "arbitrary"),
    )(proj, proj, proj, proj, dt_raw, cw, cw, cw, cbias, cbias, cbias,
      lane_row(dt_bias), lane_row(a_log), dsk, norm_w.astype(F32).reshape(1, -1))


def _final_kernel(x_ref, nw_ref, o_ref):
    seq = o_ref.shape[1]
    step = 256
    for r0 in range(0, seq, step):
        x = x_ref[0, PAD + N_META + r0:PAD + N_META + r0 + step, :]
        o_ref[0, r0:r0 + step, :] = _rms_rows(x, nw_ref[...])


def _final_norm(h3, norm_w, seq):
    bsz, tp, _ = h3.shape
    return pl.pallas_call(
        _final_kernel,
        grid=(bsz,),
        in_specs=[pl.BlockSpec((1, tp, D_MODEL), lambda b: (b, 0, 0)), _resident((1, D_MODEL))],
        out_specs=pl.BlockSpec((1, seq, D_MODEL), lambda b: (b, 0, 0)),
        out_shape=jax.ShapeDtypeStruct((bsz, seq, D_MODEL), F32),
        compiler_params=_cparams("parallel"),
    )(h3, norm_w.reshape(1, D_MODEL))


def _even_mixer(h2, bsz, tp, norm_w, w_in, rpb, conv_w, conv_b, gate_bias, ml_norm_w, w_out):
    proj, gates = _norm_proj(h2, norm_w, w_in, EV_MAIN)
    proj = proj.reshape(bsz, tp, EV_MAIN)
    y_a = _natten(proj, _na_bias_table(rpb))
    y_b = _mlstm(proj, gates.reshape(bsz, tp, LANES), conv_w, conv_b, gate_bias, ml_norm_w)
    return _out_proj([y_a.reshape(bsz * tp, D_A), y_b.reshape(bsz * tp, D_B)],
                     [w_out[:D_A], w_out[D_A:]], h2)


def _odd_mixer(h2, bsz, tp, norm_w, w_in, conv_w, conv_b, dt_bias, a_log, d_skip, ssd_norm_w, w_out):
    proj, dt_raw = _norm_proj(h2, norm_w, w_in, OD_MAIN)
    y = _ssd(proj.reshape(bsz, tp, OD_MAIN), dt_raw.reshape(bsz, tp, LANES), conv_w, conv_b,
             dt_bias, a_log, d_skip, ssd_norm_w)
    return _out_proj([y.reshape(bsz * tp, D_INNER)], [w_out], h2)


def kernel(x, meta_tokens, ffn1_norm, ffn1_gate, ffn1_up, ffn1_down, mix_norm, ffn2_norm, ffn2_gate, ffn2_up, ffn2_down, ev_w_in, na_rpb, ml_conv_w, ml_conv_b, ml_gate_bias, ml_norm_w, ev_w_out, od_w_in, ssd_conv_w, ssd_conv_b, ssd_dt_bias, ssd_a_log, ssd_d, ssd_norm_w, od_w_out, final_norm):
    bsz, seq, _ = x.shape
    tp = PAD + N_META + seq
    assert x.shape[2] == D_MODEL and seq % GRID_W == 0 and tp % ROW_TILE == 0
    assert seq // GRID_W >= NA_WIN_H
    head = jnp.concatenate([jnp.zeros((PAD, D_MODEL), F32), meta_tokens.astype(F32)], axis=0)
    h = jnp.concatenate([jnp.broadcast_to(head[None], (bsz, PAD + N_META, D_MODEL)), x], axis=1)
    h2 = h.reshape(bsz * tp, D_MODEL)
    for layer in range(ffn1_norm.shape[0]):
        h2 = _ffn(h2, ffn1_norm[layer], ffn1_gate[layer], ffn1_up[layer], ffn1_down[layer])
        i = layer // 2
        if layer % 2 == 0:
            h2 = _even_mixer(h2, bsz, tp, mix_norm[layer], ev_w_in[i], na_rpb[i], ml_conv_w[i],
                             ml_conv_b[i], ml_gate_bias[i], ml_norm_w[i], ev_w_out[i])
        else:
            h2 = _odd_mixer(h2, bsz, tp, mix_norm[layer], od_w_in[i], ssd_conv_w[i], ssd_conv_b[i],
                            ssd_dt_bias[i], ssd_a_log[i], ssd_d[i], ssd_norm_w[i], od_w_out[i])
        h2 = _ffn(h2, ffn2_norm[layer], ffn2_gate[layer], ffn2_up[layer], ffn2_down[layer])
    return _final_norm(h2.reshape(bsz, tp, D_MODEL), final_norm, seq)
```

```python
import functools

import jax
import jax.numpy as jnp
import numpy as np
from jax import lax
from jax.experimental import pallas as pl
from jax.experimental.pallas import tpu as pltpu

F32 = jnp.float32
BF16 = jnp.bfloat16

D_MODEL = 1024
N_META = 16
GRID_W = 64
CHUNK = 64
PAD = CHUNK - N_META
CONV_W = 5
CONV_HALF = (CONV_W - 1) // 2
D_FF = 2816
RMS_EPS = 1e-6

NA_HEADS = 8
NA_HEAD_DIM = 64
NA_WIN_H = 8
NA_WIN_W = 16
ML_HEADS = 4
ML_HEAD_DIM = 128
D_A = NA_HEADS * NA_HEAD_DIM
D_B = ML_HEADS * ML_HEAD_DIM
EV_MAIN = 3 * D_A + 4 * D_B

SSD_HEAD_DIM = 64
SSD_HEADS = 32
SSD_GROUPS = 8
SSD_HPG = SSD_HEADS // SSD_GROUPS
SSD_STATE = 128
D_INNER = SSD_HEADS * SSD_HEAD_DIM
SSD_GW = SSD_HPG * SSD_HEAD_DIM
OD_MAIN = 2 * D_INNER + 2 * SSD_GROUPS * SSD_STATE

LANES = 128
HALF = LANES // 2
FF_CHUNK = 256
ROW_TILE = 528
FFN_ROW_TILE = 1056
SCAN_UNROLL = 11
EXPAND_STEPS = 3
VMEM_LIMIT = 56 * 1024 * 1024


def _cparams(*sem):
    return pltpu.CompilerParams(dimension_semantics=sem, vmem_limit_bytes=VMEM_LIMIT)


def _resident(shape):
    nd = len(shape)
    return pl.BlockSpec(shape, lambda *_: (0,) * nd, pipeline_mode=pl.Buffered(1))


def _rms_rows(x, w_row):
    ms = jnp.mean(x * x, axis=-1, keepdims=True)
    return x * lax.rsqrt(ms + RMS_EPS) * w_row


def _sigmoid(x):
    return 1.0 / (1.0 + jnp.exp(-x))


def _silu(x):
    return x * _sigmoid(x)


def _softplus(x):
    return jnp.maximum(x, 0.0) + jnp.log(1.0 + jnp.exp(-jnp.abs(x)))


def _log_sigmoid(x):
    return -_softplus(-x)


def _nt(a, b):
    return lax.dot_general(a, b, (((1,), (1,)), ((), ())), preferred_element_type=F32)


def _tn(a, b):
    return lax.dot_general(a, b, (((0,), (0,)), ((), ())), preferred_element_type=F32)


def _ffn_kernel(x_ref, nw_ref, wgu_ref, wd_ref, o_ref, xn_ref, acc_ref):
    xn_ref[...] = _rms_rows(x_ref[...], nw_ref[...]).astype(BF16)
    acc_ref[...] = jnp.zeros_like(acc_ref)

    def body(f, carry):
        gu = jnp.dot(xn_ref[...], wgu_ref[f], preferred_element_type=F32)
        act = (_silu(gu[:, :FF_CHUNK]) * gu[:, FF_CHUNK:]).astype(BF16)
        acc_ref[...] += jnp.dot(act, wd_ref[f], preferred_element_type=F32)
        return carry

    lax.fori_loop(0, wgu_ref.shape[0], body, 0, unroll=True)
    o_ref[...] = x_ref[...] + 0.5 * acc_ref[...]


def _ffn(h2, norm_w, w_gate, w_up, w_down):
    rows = h2.shape[0]
    nf = D_FF // FF_CHUNK
    wg = w_gate.astype(BF16).reshape(D_MODEL, nf, FF_CHUNK)
    wu = w_up.astype(BF16).reshape(D_MODEL, nf, FF_CHUNK)
    wgu = jnp.concatenate([wg, wu], axis=-1).transpose(1, 0, 2)
    wd = w_down.astype(BF16).reshape(nf, FF_CHUNK, D_MODEL)
    row_spec = pl.BlockSpec((FFN_ROW_TILE, D_MODEL), lambda i: (i, 0))
    return pl.pallas_call(
        _ffn_kernel,
        grid=(rows // FFN_ROW_TILE,),
        in_specs=[row_spec, _resident((1, D_MODEL)), _resident(wgu.shape), _resident(wd.shape)],
        out_specs=row_spec,
        out_shape=jax.ShapeDtypeStruct(h2.shape, F32),
        scratch_shapes=[pltpu.VMEM((FFN_ROW_TILE, D_MODEL), BF16), pltpu.VMEM((FFN_ROW_TILE, D_MODEL), F32)],
        compiler_params=_cparams("parallel"),
    )(h2, norm_w.reshape(1, D_MODEL), wgu, wd)


def _norm_proj_kernel(x_ref, nw_ref, w_ref, ws_ref, o_ref, os_ref, xn_ref, *, col_chunk):
    xn_ref[...] = _rms_rows(x_ref[...], nw_ref[...]).astype(BF16)
    for c in range(w_ref.shape[1] // col_chunk):
        cols = slice(c * col_chunk, (c + 1) * col_chunk)
        o_ref[:, cols] = jnp.dot(xn_ref[...], w_ref[:, cols], preferred_element_type=F32).astype(BF16)
    os_ref[...] = jnp.dot(xn_ref[...], ws_ref[...], preferred_element_type=F32)


def _norm_proj(h2, norm_w, w_in, n_main):
    rows = h2.shape[0]
    w_main = w_in[:, :n_main].astype(BF16)
    n_small = w_in.shape[1] - n_main
    w_small = jnp.pad(w_in[:, n_main:], ((0, 0), (0, LANES - n_small))).astype(BF16)
    row_spec = pl.BlockSpec((ROW_TILE, D_MODEL), lambda i: (i, 0))
    return pl.pallas_call(
        functools.partial(_norm_proj_kernel, col_chunk=2 * FF_CHUNK),
        grid=(rows // ROW_TILE,),
        in_specs=[row_spec, _resident((1, D_MODEL)), _resident(w_main.shape), _resident(w_small.shape)],
        out_specs=[pl.BlockSpec((ROW_TILE, n_main), lambda i: (i, 0)),
                   pl.BlockSpec((ROW_TILE, LANES), lambda i: (i, 0))],
        out_shape=[jax.ShapeDtypeStruct((rows, n_main), BF16),
                   jax.ShapeDtypeStruct((rows, LANES), F32)],
        scratch_shapes=[pltpu.VMEM((ROW_TILE, D_MODEL), BF16)],
        compiler_params=_cparams("parallel"),
    )(h2, norm_w.reshape(1, D_MODEL), w_main, w_small)


def _out_proj_kernel(*refs):
    n = (len(refs) - 2) // 2
    h_ref, o_ref = refs[2 * n], refs[2 * n + 1]
    acc = h_ref[...]
    for y_ref, w_ref in zip(refs[:n], refs[n:2 * n]):
        acc = acc + jnp.dot(y_ref[...], w_ref[...], preferred_element_type=F32)
    o_ref[...] = acc


def _out_proj(ys, ws, h2):
    rows = h2.shape[0]
    row_spec = pl.BlockSpec((ROW_TILE, D_MODEL), lambda i: (i, 0))
    ws = [w.astype(BF16) for w in ws]
    return pl.pallas_call(
        _out_proj_kernel,
        grid=(rows // ROW_TILE,),
        in_specs=([pl.BlockSpec((ROW_TILE, y.shape[1]), lambda i: (i, 0)) for y in ys]
                  + [_resident(w.shape) for w in ws] + [row_spec]),
        out_specs=row_spec,
        out_shape=jax.ShapeDtypeStruct(h2.shape, F32),
        compiler_params=_cparams("parallel"),
    )(*ys, *ws, h2)


def _na_bias_table(rpb):
    qc = np.arange(GRID_W)[:, None]
    kc = np.arange(GRID_W)[None, :]
    dc = np.clip(kc - qc, 1 - NA_WIN_W, NA_WIN_W - 1) + (NA_WIN_W - 1)
    win_c0 = np.clip(qc - NA_WIN_W // 2, 0, GRID_W - NA_WIN_W)
    col_ok = (kc >= win_c0) & (kc < win_c0 + NA_WIN_W)
    onehot = (dc[:, :, None] == np.arange(2 * NA_WIN_W - 1)).astype(np.float32)
    toep = jnp.einsum('hdc,qkc->hdqk', rpb.astype(F32), onehot, precision=lax.Precision.HIGHEST)
    toep = jnp.where(col_ok[None, None], toep, -jnp.inf)
    tbl = jnp.stack([toep[:, d0:d0 + NA_WIN_H] for d0 in range(NA_WIN_H)], axis=1)
    tbl = tbl.transpose(0, 1, 3, 2, 4).reshape(NA_HEADS, NA_WIN_H, GRID_W, NA_WIN_H * GRID_W)
    meta = jnp.zeros((NA_HEADS, NA_WIN_H, GRID_W, N_META), F32)
    return jnp.concatenate([tbl, meta], axis=-1)


def _natten_kernel(q_ref, k_ref, v_ref, bias_ref, o_ref, *, n_rows):
    scale = NA_HEAD_DIM ** -0.5
    g0 = PAD + N_META
    n_keys = NA_WIN_H * GRID_W

    pairs = [slice(j * LANES, (j + 1) * LANES) for j in range(NA_HEADS // 2)]

    def attend(q, k, v, bias):
        low = lax.broadcasted_iota(jnp.int32, (q.shape[0], LANES), 1) < NA_HEAD_DIM
        s = []
        for c in pairs:
            qp = q[:, c]
            s.append(_nt(jnp.where(low, qp, jnp.zeros_like(qp)), k[:, c]))
            s.append(_nt(jnp.where(low, jnp.zeros_like(qp), qp), k[:, c]))
        p, l = [], []
        for h in range(NA_HEADS):
            sh = s[h] * scale if bias is None else s[h] * scale + bias(h)
            ph = jnp.exp(sh - jnp.max(sh, axis=1, keepdims=True))
            l.append(jnp.sum(ph, axis=1, keepdims=True))
            p.append(ph.astype(BF16))
        o = [jnp.dot(p[h], v[:, pairs[h // 2]], preferred_element_type=F32) for h in range(NA_HEADS)]
        return jnp.concatenate([jnp.where(low, o[2 * j] / l[2 * j], o[2 * j + 1] / l[2 * j + 1])
                                for j in range(NA_HEADS // 2)], axis=1).astype(BF16)

    k_meta = k_ref[0, PAD:g0, :]
    v_meta = v_ref[0, PAD:g0, :]
    o_ref[0, :PAD, :] = jnp.zeros((PAD, D_A), BF16)
    o_ref[0, PAD:g0, :] = attend(q_ref[0, PAD:g0, :], k_meta, v_meta, None)

    def row_body(r, carry):
        first = jnp.clip(r - NA_WIN_H // 2, 0, n_rows - NA_WIN_H)
        d0 = first - r + (NA_WIN_H - 1)
        q_rows = pl.ds(pl.multiple_of(g0 + r * GRID_W, GRID_W), GRID_W)
        k_rows = pl.ds(pl.multiple_of(g0 + first * GRID_W, GRID_W), n_keys)
        k = jnp.concatenate([k_ref[0, k_rows, :], k_meta], axis=0)
        v = jnp.concatenate([v_ref[0, k_rows, :], v_meta], axis=0)
        o_ref[0, q_rows, :] = attend(q_ref[0, q_rows, :], k, v, lambda h: bias_ref[h, d0])
        return carry

    lax.fori_loop(0, n_rows, row_body, 0)


def _natten(proj, bias_tbl):
    bsz, tp, _ = proj.shape
    n_rows = (tp - PAD - N_META) // GRID_W

    def col_block(j):
        return pl.BlockSpec((1, tp, D_A), lambda b: (b, 0, j))

    return pl.pallas_call(
        functools.partial(_natten_kernel, n_rows=n_rows),
        grid=(bsz,),
        in_specs=[col_block(0), col_block(1), col_block(2), _resident(bias_tbl.shape)],
        out_specs=pl.BlockSpec((1, tp, D_A), lambda b: (b, 0, 0)),
        out_shape=jax.ShapeDtypeStruct((bsz, tp, D_A), BF16),
        compiler_params=_cparams("parallel"),
    )(proj, proj, proj, bias_tbl)


def _fill_conv_src(src_ref, parts, tp):
    width = src_ref.shape[1]
    src_ref[0:8 + PAD, :] = jnp.zeros((8 + PAD, width), F32)
    src_ref[8 + tp:16 + tp, :] = jnp.zeros((8, width), F32)
    off = 0
    for p in parts:
        w = p.shape[2]
        src_ref[8 + PAD:8 + tp, off:off + w] = p[0, PAD:tp, :].astype(F32)
        off += w


def _conv_silu(src_ref, w, b, c):
    x = src_ref[pl.ds(pl.multiple_of(c * CHUNK, CHUNK), CHUNK + 16), :]
    acc = b + w[0:1, :] * x[8 - CONV_HALF:8 - CONV_HALF + CHUNK]
    for j in range(1, CONV_W):
        lo = 8 - CONV_HALF + j
        acc = acc + w[j:j + 1, :] * x[lo:lo + CHUNK]
    return _silu(acc)


def _chunk_rows(c):
    return pl.ds(pl.multiple_of(c * CHUNK, CHUNK), CHUNK)


def _scan_steps(i, n_chunks):
    cs = {}
    for u in range(SCAN_UNROLL):
        cs[u, 0] = i * SCAN_UNROLL + u
        cs[u, 1] = n_chunks - 1 - cs[u, 0]
    return cs, {k: _chunk_rows(c) for k, c in cs.items()}


def _scan_tri(shape, reverse):
    t = lax.broadcasted_iota(jnp.int32, shape, 0)
    s = lax.broadcasted_iota(jnp.int32, shape, 1) % CHUNK
    return (s >= t) if reverse else (s <= t)


def _cumsum_rows(x):
    row = lax.broadcasted_iota(jnp.int32, x.shape, 0)
    sh = 1
    while sh < CHUNK:
        x = x + jnp.where(row >= sh, pltpu.roll(x, sh, axis=0), 0.0)
        sh *= 2
    return x


def _pack3(x):
    hi = x.astype(BF16).astype(F32)
    r1 = x - hi
    mid = r1.astype(BF16).astype(F32)
    return jnp.concatenate([hi + pltpu.roll(mid, HALF, axis=1), r1 - mid], axis=1).astype(BF16)


def _pack2(x):
    hi = x.astype(BF16)
    return jnp.concatenate([hi, (x - hi.astype(F32)).astype(BF16)], axis=1)


def _selector(lane_of_row, terms):
    k = lax.broadcasted_iota(jnp.int32, lane_of_row.shape, 1)
    hit = (k == lane_of_row) | (k == lane_of_row + LANES)
    if terms == 3:
        hit = hit | (k == lane_of_row + HALF)
    return jnp.where(hit, 1.0, 0.0).astype(BF16)


def _mlstm_kernel(q_ref, k_ref, v_ref, og_ref, gates_ref, cwq_ref, cwk_ref, cbq_ref, cbk_ref,
                  gb_ref, nw_ref, o_ref, src_ref, qc_ref, kc_ref, x_ref, lf_ref, gf_ref,
                  stat_ref, m_ref, a_ref, wi_ref, em_ref, wv_ref, wo_ref, y_ref, st_ref):
    tp = q_ref.shape[1]
    n_chunks = tp // CHUNK
    head = pl.program_id(1)
    hd = ML_HEAD_DIM

    _fill_conv_src(src_ref, (q_ref, k_ref), tp)
    row = lax.broadcasted_iota(jnp.int32, (tp, LANES), 0)
    x_all = jnp.where(row < PAD, 0.0, gates_ref[0] + gb_ref[...])
    x_ref[...] = x_all
    lf_ref[...] = _log_sigmoid(x_all)
    st_ref[...] = jnp.zeros_like(st_ref)
    y_ref[...] = jnp.zeros_like(y_ref)

    conv_w = jnp.concatenate([cwq_ref[...], cwk_ref[...]], axis=1)
    conv_b = jnp.concatenate([cbq_ref[...], cbk_ref[...]], axis=1)

    def loop_a(c, carry):
        rows = _chunk_rows(c)
        qk = _conv_silu(src_ref, conv_w, conv_b, c)
        qc_ref[rows, :] = qk[:, :hd].astype(BF16)
        kc_ref[rows, :] = (qk[:, hd:] * (hd ** -0.5)).astype(BF16)
        lf = lf_ref[rows, :]
        gf = _cumsum_rows(lf)
        gf_ref[rows, :] = gf
        xi = pltpu.roll(x_ref[rows, :], ML_HEADS, axis=1)
        tot = gf[CHUNK - 1:CHUNK]
        gb = tot - gf + lf
        stat_ref[c] = jnp.concatenate(
            [tot, jnp.max(tot - gf + xi, axis=0, keepdims=True),
             tot, jnp.max(tot - gb + xi, axis=0, keepdims=True),
             jnp.zeros((4, LANES), F32)], axis=0)
        return carry

    lax.fori_loop(0, n_chunks, loop_a, 0)
    qc_ref[0:PAD, :] = jnp.zeros((PAD, hd), BF16)
    kc_ref[0:PAD, :] = jnp.zeros((PAD, hd), BF16)

    def loop_m(i, m):
        mf, mb = m
        sf = stat_ref[i]
        nf = jnp.maximum(sf[0:1] + mf, sf[1:2])
        m_ref[0, i] = jnp.concatenate([mf, nf, jnp.zeros((6, LANES), F32)], axis=0)
        cb = n_chunks - 1 - i
        sb = stat_ref[cb]
        nb = jnp.maximum(sb[2:3] + mb, sb[3:4])
        m_ref[1, cb] = jnp.concatenate([mb, nb, jnp.zeros((6, LANES), F32)], axis=0)
        return nf, nb

    zero_row = jnp.zeros((1, LANES), F32)
    lax.fori_loop(0, n_chunks, loop_m, (zero_row, zero_row))

    lane = lax.broadcasted_iota(jnp.int32, (CHUNK, LANES), 1)
    lane1 = lax.broadcasted_iota(jnp.int32, (1, LANES), 1)
    vrow = lax.broadcasted_iota(jnp.int32, (CHUNK, hd), 0)

    def pick(x, lane_ids, f_lane):
        return jnp.sum(jnp.where(lane_ids == f_lane, x, 0.0), axis=1, keepdims=True)

    units = [(u, d) for u in range(SCAN_UNROLL) for d in range(2)]
    f_lanes = [2 * ML_HEADS * d + ML_HEADS + head for d in range(2)]
    sels = [_selector(jnp.full((8, 2 * LANES), f, jnp.int32), 3) for f in f_lanes]

    def loop_p(i, carry):
        cs = [i * SCAN_UNROLL + u for u in range(SCAN_UNROLL)]
        rows = [_chunk_rows(c) for c in cs]
        qk = [_nt(qc_ref[r, :], kc_ref[r, :]) for r in rows]
        v1, g_col, w_col, w_pk = [], {}, {}, {}
        for u in range(SCAN_UNROLL):
            v = jnp.where(vrow + cs[u] * CHUNK < PAD, 0.0, v_ref[0, rows[u], :].astype(F32))
            v1.append(jnp.concatenate([v, jnp.ones((CHUNK, hd), F32)], axis=1))
            lf = lf_ref[rows[u], :]
            gf = gf_ref[rows[u], :]
            xi = pltpu.roll(x_ref[rows[u], :], ML_HEADS, axis=1)
            for d in range(2):
                g_all = (gf[CHUNK - 1:CHUNK] - gf + lf) if d == 1 else gf
                g_col[u, d] = pick(g_all, lane, f_lanes[d])
                w_all = jnp.where(lane == f_lanes[d], xi - g_all, 0.0)
                w_col[u, d] = jnp.sum(w_all, axis=1, keepdims=True)
                w_pk[u, d] = _pack3(w_all)
        w_row = {(u, d): _nt(sels[d], w_pk[u, d]) for u, d in units}
        sw = {}
        for u, d in units:
            reverse = d == 1
            gc = g_col[u, d]
            dmat = gc + jnp.concatenate([w_row[u, d]] * (CHUNK // 8), axis=0)
            dmat = jnp.where(_scan_tri((CHUNK, CHUNK), reverse), dmat, -jnp.inf)
            mm = m_ref[d, cs[u]]
            m_st = pick(mm[0:1], lane1, f_lanes[d])
            m_new = pick(mm[1:2], lane1, f_lanes[d])
            g_last = gc[0:1] if reverse else gc[CHUNK - 1:CHUNK]
            m_inter = gc + m_st
            m_t = jnp.maximum(m_inter, jnp.max(dmat, axis=1, keepdims=True))
            sw[u, d] = (jnp.exp(dmat - m_t) * qk[u]).astype(BF16)
            wi_ref[d, rows[u], :] = jnp.broadcast_to(jnp.exp(m_inter - m_t), (CHUNK, LANES))
            em_ref[d, rows[u], :] = jnp.broadcast_to(jnp.exp(-m_t), (CHUNK, LANES))
            w_end = jnp.exp(g_last + w_col[u, d] - m_new)
            wv_ref[d, rows[u], :] = (w_end * v1[u]).astype(BF16)
            wo_ref[d, cs[u]] = jnp.broadcast_to(jnp.exp(g_last + m_st - m_new), (8, LANES))
        v1b = [x.astype(BF16) for x in v1]
        intra = {(u, d): jnp.dot(sw[u, d], v1b[u], preferred_element_type=F32) for u, d in units}
        for u, d in units:
            a_ref[d, rows[u], :] = intra[u, d]
        return carry

    lax.fori_loop(0, n_chunks // SCAN_UNROLL, loop_p, 0)

    def loop_s(i, carry):
        cs, rows = _scan_steps(i, n_chunks)
        upd = {k: _tn(kc_ref[rows[k], :], wv_ref[k[1], rows[k], :]) for k in units}
        st = {(0, d): st_ref[d] for d in range(2)}
        inter = {}
        for u in range(SCAN_UNROLL):
            for d in range(2):
                inter[u, d] = jnp.dot(qc_ref[rows[u, d], :], st[u, d].astype(BF16),
                                      preferred_element_type=F32)
                st[u + 1, d] = wo_ref[d, cs[u, d]][0:1, 0:1] * st[u, d] + upd[u, d]
        for u, d in units:
            r = rows[u, d]
            wi = wi_ref[d, r, :]
            tot = a_ref[d, r, :] + jnp.concatenate([wi, wi], axis=1) * inter[u, d]
            y_ref[r, :] += tot[:, :hd] / jnp.maximum(jnp.abs(tot[:, hd:]), em_ref[d, r, :])
        for d in range(2):
            st_ref[d] = st[SCAN_UNROLL, d]
        return carry

    lax.fori_loop(0, n_chunks // SCAN_UNROLL, loop_s, 0)

    y = y_ref[...]
    y = y * lax.rsqrt(jnp.mean(y * y, axis=1, keepdims=True) + RMS_EPS) * nw_ref[...]
    y = y * _sigmoid(og_ref[0].astype(F32))
    o_ref[0] = jnp.where(row < PAD, 0.0, y).astype(BF16)


def _mlstm(proj, gates, conv_w, conv_b, gate_bias, norm_w):
    bsz, tp, _ = proj.shape
    hd = ML_HEAD_DIM
    nc = tp // CHUNK
    base = 3 * D_A // hd

    def col_block(j0):
        return pl.BlockSpec((1, tp, hd), lambda b, h: (b, 0, j0 + h))

    def par_block(nrow, j0):
        return pl.BlockSpec((nrow, hd), lambda b, h: (0, j0 + h))

    gb = jnp.pad(gate_bias.astype(F32), (0, LANES - gate_bias.shape[0])).reshape(1, LANES)
    return pl.pallas_call(
        _mlstm_kernel,
        grid=(bsz, ML_HEADS),
        in_specs=[col_block(base), col_block(base + ML_HEADS), col_block(base + 2 * ML_HEADS),
                  col_block(base + 3 * ML_HEADS),
                  pl.BlockSpec((1, tp, LANES), lambda b, h: (b, 0, 0)),
                  par_block(CONV_W, 0), par_block(CONV_W, ML_HEADS),
                  par_block(1, 0), par_block(1, ML_HEADS),
                  pl.BlockSpec((1, LANES), lambda b, h: (0, 0)),
                  par_block(1, 0)],
        out_specs=pl.BlockSpec((1, tp, hd), lambda b, h: (b, 0, h)),
        out_shape=jax.ShapeDtypeStruct((bsz, tp, D_B), BF16),
        scratch_shapes=[pltpu.VMEM((tp + 16, 2 * hd), F32),
                        pltpu.VMEM((tp, hd), BF16),
                        pltpu.VMEM((tp, hd), BF16),
                        pltpu.VMEM((tp, LANES), F32),
                        pltpu.VMEM((tp, LANES), F32),
                        pltpu.VMEM((tp, LANES), F32),
                        pltpu.VMEM((nc, 8, LANES), F32),
                        pltpu.VMEM((2, nc, 8, LANES), F32),
                        pltpu.VMEM((2, tp, 2 * hd), F32),
                        pltpu.VMEM((2, tp, LANES), F32),
                        pltpu.VMEM((2, tp, LANES), F32),
                        pltpu.VMEM((2, tp, 2 * hd), BF16),
                        pltpu.VMEM((2, nc, 8, LANES), F32),
                        pltpu.VMEM((tp, hd), F32),
                        pltpu.VMEM((2, hd, 2 * hd), F32)],
        compiler_params=_cparams("parallel", "arbitrary"),
    )(proj, proj, proj, proj, gates, conv_w.astype(F32), conv_w.astype(F32),
      conv_b.astype(F32).reshape(1, -1), conv_b.astype(F32).reshape(1, -1), gb,
      norm_w.astype(F32).reshape(1, -1))


def _ssd_kernel(z_ref, x_ref, b_ref, c_ref, dt_ref, cwx_ref, cwb_ref, cwc_ref, cbx_ref, cbb_ref,
                cbc_ref, dtb_ref, alog_ref, dsk_ref, nw_ref, o_ref,
                src_ref, xs_ref, bc_ref, cc_ref, dts_ref, da_ref, cum_ref, sel_ref, rsel_ref,
                pk_ref, cum4_ref, xdt_ref, xw_ref, ecl_ref, y_ref, st_ref):
    tp = x_ref.shape[1]
    n_chunks = tp // CHUNK
    assert n_chunks % SCAN_UNROLL == 0
    grp = pl.program_id(1)
    gw = SSD_GW
    ns = SSD_STATE
    hdim = SSD_HEAD_DIM

    _fill_conv_src(src_ref, (x_ref, b_ref, c_ref), tp)
    row = lax.broadcasted_iota(jnp.int32, (tp, LANES), 0)
    lane_all = lax.broadcasted_iota(jnp.int32, (tp, LANES), 1)
    dt_all = jnp.where((row < PAD) | (lane_all >= 2 * SSD_HEADS), 0.0,
                       _softplus(dt_ref[0] + dtb_ref[...]))
    dts_ref[...] = dt_all
    da_ref[...] = dt_all * (-jnp.exp(alog_ref[...]))
    st_ref[...] = jnp.zeros_like(st_ref)

    col_head = lax.broadcasted_iota(jnp.int32, (gw, 2 * LANES), 0) // hdim
    row_head = lax.broadcasted_iota(jnp.int32, (8 * SSD_HPG, 2 * LANES), 0) // 8
    for d in range(2):
        lane0 = SSD_HEADS * d + SSD_HPG * grp
        sel_ref[d, 0] = _selector(lane0 + col_head, 3)
        sel_ref[d, 1] = _selector(lane0 + col_head, 2)
        rsel_ref[d] = _selector(lane0 + row_head, 3)

    conv_w = jnp.concatenate([cwx_ref[...], cwb_ref[...], cwc_ref[...]], axis=1)
    conv_b = jnp.concatenate([cbx_ref[...], cbb_ref[...], cbc_ref[...]], axis=1)

    def loop_a(c, carry):
        rows = _chunk_rows(c)
        a = _conv_silu(src_ref, conv_w, conv_b, c)
        xs_ref[rows, :] = a[:, :gw]
        bc_ref[rows, :] = a[:, gw:gw + ns].astype(BF16)
        cc_ref[rows, :] = a[:, gw + ns:].astype(BF16)
        da = da_ref[rows, :]
        cumf = _cumsum_rows(da)
        cum_ref[0, rows, :] = cumf
        cum_ref[1, rows, :] = cumf[CHUNK - 1:CHUNK] - cumf + da
        return carry

    lax.fori_loop(0, n_chunks, loop_a, 0)
    xs_ref[0:PAD, :] = jnp.zeros((PAD, gw), F32)
    bc_ref[0:PAD, :] = jnp.zeros((PAD, ns), BF16)
    cc_ref[0:PAD, :] = jnp.zeros((PAD, ns), BF16)

    eb_rows = tp // EXPAND_STEPS

    def loop_e(i, carry):
        rows = pl.ds(pl.multiple_of(i * eb_rows, CHUNK), eb_rows)
        pk = [_pack3(cum_ref[d, rows, :]) for d in range(2)]
        pk_dt = _pack2(dts_ref[rows, :])
        cum4 = [_nt(pk[d], sel_ref[d, 0]) for d in range(2)]
        dt4 = [_nt(pk_dt, sel_ref[d, 1]) for d in range(2)]
        xs = xs_ref[rows, :]
        for d in range(2):
            pk_ref[d, rows, :] = pk[d]
            cum4_ref[d, rows, :] = cum4[d]
            xdt_ref[d, rows, :] = xs * dt4[d]
        return carry

    lax.fori_loop(0, EXPAND_STEPS, loop_e, 0)

    blk4 = lax.broadcasted_iota(jnp.int32, (CHUNK, gw), 1) // hdim
    units = [(u, d) for u in range(SCAN_UNROLL) for d in range(2)]

    def loop_p(i, carry):
        cs = [i * SCAN_UNROLL + u for u in range(SCAN_UNROLL)]
        rows = [_chunk_rows(c) for c in cs]
        cb2 = [_nt(cc_ref[r, :], jnp.concatenate([bc_ref[r, :]] * 2, axis=0)) for r in rows]
        cb4 = [jnp.concatenate([x] * (SSD_HPG // 2), axis=1) for x in cb2]
        row32 = {(u, d): _nt(rsel_ref[d], pk_ref[d, rows[u], :]) for u, d in units}
        m4, xdiag = {}, {}
        for u, d in units:
            reverse = d == 1
            cum4 = cum4_ref[d, rows[u], :]
            r32 = row32[u, d]
            row_cum = jnp.concatenate([r32[8 * r:8 * r + 8] for r in range(SSD_HPG)], axis=1)
            seg4 = cum4 - jnp.concatenate([row_cum] * (CHUNK // 8), axis=0)
            decay4 = jnp.where(_scan_tri((CHUNK, gw), reverse), jnp.exp(seg4), 0.0)
            m4[u, d] = (decay4 * cb4[u]).astype(BF16)
            xdt = xdt_ref[d, rows[u], :]
            xdt_b = xdt.astype(BF16)
            xdiag[u, d] = jnp.concatenate([jnp.where(blk4 == r, xdt_b, jnp.zeros_like(xdt_b))
                                           for r in range(SSD_HPG)], axis=0)
            cum_last = cum4[0:1] if reverse else cum4[CHUNK - 1:CHUNK]
            xw_ref[d, rows[u], :] = (jnp.exp(cum_last - cum4) * xdt).astype(BF16)
            ecl_ref[d, cs[u]] = jnp.broadcast_to(jnp.exp(cum_last), (8, gw))
        intra = {k: jnp.dot(m4[k], xdiag[k], preferred_element_type=F32) for k in units}
        for u in range(SCAN_UNROLL):
            y_ref[rows[u], :] = dsk_ref[...] * xs_ref[rows[u], :] + intra[u, 0] + intra[u, 1]
        return carry

    lax.fori_loop(0, n_chunks // SCAN_UNROLL, loop_p, 0)

    def loop_s(i, carry):
        cs, rows = _scan_steps(i, n_chunks)
        upd = {k: _tn(bc_ref[rows[k], :], xw_ref[k[1], rows[k], :]) for k in units}
        st = {(0, d): st_ref[d] for d in range(2)}
        inter = {}
        for u in range(SCAN_UNROLL):
            for d in range(2):
                inter[u, d] = jnp.dot(cc_ref[rows[u, d], :], st[u, d].astype(BF16),
                                      preferred_element_type=F32)
                st[u + 1, d] = ecl_ref[d, cs[u, d]][0:1] * st[u, d] + upd[u, d]
        for u, d in units:
            y_ref[rows[u, d], :] += jnp.exp(cum4_ref[d, rows[u, d], :]) * inter[u, d]
        for d in range(2):
            st_ref[d] = st[SCAN_UNROLL, d]
        return carry

    lax.fori_loop(0, n_chunks // SCAN_UNROLL, loop_s, 0)

    rowg = lax.broadcasted_iota(jnp.int32, (tp, gw), 0)
    y = y_ref[...] * _silu(z_ref[0].astype(F32))
    y = y * lax.rsqrt(jnp.mean(y * y, axis=1, keepdims=True) + RMS_EPS) * nw_ref[...]
    o_ref[0] = jnp.where(rowg < PAD, 0.0, y).astype(BF16)


def _ssd(proj, dt_raw, conv_w, conv_b, dt_bias, a_log, d_skip, norm_w):
    bsz, tp, _ = proj.shape
    gw, ns = SSD_GW, SSD_STATE
    nc = tp // CHUNK
    xb, bb, cb = D_INNER // gw, 2 * D_INNER // ns, 2 * D_INNER // ns + SSD_GROUPS

    def col_block(width, j0):
        return pl.BlockSpec((1, tp, width), lambda b, g: (b, 0, j0 + g))

    def par_block(nrow, width, j0):
        return pl.BlockSpec((nrow, width), lambda b, g: (0, j0 + g))

    def lane_row(v):
        v = v.astype(F32).reshape(-1)
        return jnp.pad(v, (0, LANES - v.shape[0])).reshape(1, LANES)

    cw = conv_w.astype(F32)
    cbias = conv_b.astype(F32).reshape(1, -1)
    xb_c, bb_c, cb_c = 0, D_INNER // ns, D_INNER // ns + SSD_GROUPS
    dsk = jnp.repeat(d_skip.astype(F32), SSD_HEAD_DIM).reshape(1, D_INNER)
    full_row = pl.BlockSpec((1, LANES), lambda b, g: (0, 0))
    return pl.pallas_call(
        _ssd_kernel,
        grid=(bsz, SSD_GROUPS),
        in_specs=[col_block(gw, 0), col_block(gw, xb), col_block(ns, bb), col_block(ns, cb),
                  pl.BlockSpec((1, tp, LANES), lambda b, g: (b, 0, 0)),
                  par_block(CONV_W, gw, xb_c), par_block(CONV_W, ns, bb_c), par_block(CONV_W, ns, cb_c),
                  par_block(1, gw, xb_c), par_block(1, ns, bb_c), par_block(1, ns, cb_c),
                  full_row, full_row, par_block(1, gw, 0), par_block(1, gw, 0)],
        out_specs=pl.BlockSpec((1, tp, gw), lambda b, g: (b, 0, g)),
        out_shape=jax.ShapeDtypeStruct((bsz, tp, D_INNER), BF16),
        scratch_shapes=[pltpu.VMEM((tp + 16, gw + 2 * ns), F32),
                        pltpu.VMEM((tp, gw), F32),
                        pltpu.VMEM((tp, ns), BF16),
                        pltpu.VMEM((tp, ns), BF16),
                        pltpu.VMEM((tp, LANES), F32),
                        pltpu.VMEM((tp, LANES), F32),
                        pltpu.VMEM((2, tp, LANES), F32),
                        pltpu.VMEM((2, 2, gw, 2 * LANES), BF16),
                        pltpu.VMEM((2, 8 * SSD_HPG, 2 * LANES), BF16),
                        pltpu.VMEM((2, tp, 2 * LANES), BF16),
                        pltpu.VMEM((2, tp, gw), F32),
                        pltpu.VMEM((2, tp, gw), F32),
                        pltpu.VMEM((2, tp, gw), BF16),
                        pltpu.VMEM((2, nc, 8, gw), F32),
                        pltpu.VMEM((tp, gw), F32),
                        pltpu.VMEM((2, ns, gw), F32)],
        compiler_params=_cparams("parallel", "arbitrary"),
    )(proj, proj, proj, proj, dt_raw, cw, cw, cw, cbias, cbias, cbias,
      lane_row(dt_bias), lane_row(a_log), dsk, norm_w.astype(F32).reshape(1, -1))


def _final_kernel(x_ref, nw_ref, o_ref):
    seq = o_ref.shape[1]
    step = 256
    for r0 in range(0, seq, step):
        x = x_ref[0, PAD + N_META + r0:PAD + N_META + r0 + step, :]
        o_ref[0, r0:r0 + step, :] = _rms_rows(x, nw_ref[...])


def _final_norm(h3, norm_w, seq):
    bsz, tp, _ = h3.shape
    return pl.pallas_call(
        _final_kernel,
        grid=(bsz,),
        in_specs=[pl.BlockSpec((1, tp, D_MODEL), lambda b: (b, 0, 0)), _resident((1, D_MODEL))],
        out_specs=pl.BlockSpec((1, seq, D_MODEL), lambda b: (b, 0, 0)),
        out_shape=jax.ShapeDtypeStruct((bsz, seq, D_MODEL), F32),
        compiler_params=_cparams("parallel"),
    )(h3, norm_w.reshape(1, D_MODEL))


def _even_mixer(h2, bsz, tp, norm_w, w_in, rpb, conv_w, conv_b, gate_bias, ml_norm_w, w_out):
    proj, gates = _norm_proj(h2, norm_w, w_in, EV_MAIN)
    proj = proj.reshape(bsz, tp, EV_MAIN)
    y_a = _natten(proj, _na_bias_table(rpb))
    y_b = _mlstm(proj, gates.reshape(bsz, tp, LANES), conv_w, conv_b, gate_bias, ml_norm_w)
    return _out_proj([y_a.reshape(bsz * tp, D_A), y_b.reshape(bsz * tp, D_B)],
                     [w_out[:D_A], w_out[D_A:]], h2)


def _odd_mixer(h2, bsz, tp, norm_w, w_in, conv_w, conv_b, dt_bias, a_log, d_skip, ssd_norm_w, w_out):
    proj, dt_raw = _norm_proj(h2, norm_w, w_in, OD_MAIN)
    y = _ssd(proj.reshape(bsz, tp, OD_MAIN), dt_raw.reshape(bsz, tp, LANES), conv_w, conv_b,
             dt_bias, a_log, d_skip, ssd_norm_w)
    return _out_proj([y.reshape(bsz * tp, D_INNER)], [w_out], h2)


def kernel(x, meta_tokens, ffn1_norm, ffn1_gate, ffn1_up, ffn1_down, mix_norm, ffn2_norm, ffn2_gate, ffn2_up, ffn2_down, ev_w_in, na_rpb, ml_conv_w, ml_conv_b, ml_gate_bias, ml_norm_w, ev_w_out, od_w_in, ssd_conv_w, ssd_conv_b, ssd_dt_bias, ssd_a_log, ssd_d, ssd_norm_w, od_w_out, final_norm):
    bsz, seq, _ = x.shape
    tp = PAD + N_META + seq
    assert x.shape[2] == D_MODEL and seq % GRID_W == 0 and tp % ROW_TILE == 0
    assert seq // GRID_W >= NA_WIN_H
    head = jnp.concatenate([jnp.zeros((PAD, D_MODEL), F32), meta_tokens.astype(F32)], axis=0)
    h = jnp.concatenate([jnp.broadcast_to(head[None], (bsz, PAD + N_META, D_MODEL)), x], axis=1)
    h2 = h.reshape(bsz * tp, D_MODEL)
    for layer in range(ffn1_norm.shape[0]):
        h2 = _ffn(h2, ffn1_norm[layer], ffn1_gate[layer], ffn1_up[layer], ffn1_down[layer])
        i = layer // 2
        if layer % 2 == 0:
            h2 = _even_mixer(h2, bsz, tp, mix_norm[layer], ev_w_in[i], na_rpb[i], ml_conv_w[i],
                             ml_conv_b[i], ml_gate_bias[i], ml_norm_w[i], ev_w_out[i])
        else:
            h2 = _odd_mixer(h2, bsz, tp, mix_norm[layer], od_w_in[i], ssd_conv_w[i], ssd_conv_b[i],
                            ssd_dt_bias[i], ssd_a_log[i], ssd_d[i], ssd_norm_w[i], od_w_out[i])
        h2 = _ffn(h2, ffn2_norm[layer], ffn2_gate[layer], ffn2_up[layer], ffn2_down[layer])
    return _final_norm(h2.reshape(bsz, tp, D_MODEL), final_norm, seq)
```

```python
import functools

import jax
import jax.numpy as jnp
import numpy as np
from jax import lax
from jax.experimental import pallas as pl
from jax.experimental.pallas import tpu as pltpu

F32 = jnp.float32
BF16 = jnp.bfloat16

D_MODEL = 1024
N_META = 16
GRID_W = 64
CHUNK = 64
PAD = CHUNK - N_META
CONV_W = 5
CONV_HALF = (CONV_W - 1) // 2
D_FF = 2816
RMS_EPS = 1e-6

NA_HEADS = 8
NA_HEAD_DIM = 64
NA_WIN_H = 8
NA_WIN_W = 16
ML_HEADS = 4
ML_HEAD_DIM = 128
D_A = NA_HEADS * NA_HEAD_DIM
D_B = ML_HEADS * ML_HEAD_DIM
EV_MAIN = 3 * D_A + 4 * D_B

SSD_HEAD_DIM = 64
SSD_HEADS = 32
SSD_GROUPS = 8
SSD_HPG = SSD_HEADS // SSD_GROUPS
SSD_STATE = 128
D_INNER = SSD_HEADS * SSD_HEAD_DIM
SSD_GW = SSD_HPG * SSD_HEAD_DIM
OD_MAIN = 2 * D_INNER + 2 * SSD_GROUPS * SSD_STATE

LANES = 128
HALF = LANES // 2
FF_CHUNK = 256
ROW_TILE = 528
FFN_ROW_TILE = 1056
NA_ROWS_PER_STEP = 2
SCAN_UNROLL = 11
EXPAND_STEPS = 3
VMEM_LIMIT = 56 * 1024 * 1024


def _cparams(*sem):
    return pltpu.CompilerParams(dimension_semantics=sem, vmem_limit_bytes=VMEM_LIMIT)


def _resident(shape):
    nd = len(shape)
    return pl.BlockSpec(shape, lambda *_: (0,) * nd, pipeline_mode=pl.Buffered(1))


def _rms_rows(x, w_row):
    ms = jnp.mean(x * x, axis=-1, keepdims=True)
    return x * lax.rsqrt(ms + RMS_EPS) * w_row


def _sigmoid(x):
    return 1.0 / (1.0 + jnp.exp(-x))


def _silu(x):
    return x * _sigmoid(x)


def _softplus(x):
    return jnp.maximum(x, 0.0) + jnp.log(1.0 + jnp.exp(-jnp.abs(x)))


def _log_sigmoid(x):
    return -_softplus(-x)


def _nt(a, b):
    return lax.dot_general(a, b, (((1,), (1,)), ((), ())), preferred_element_type=F32)


def _tn(a, b):
    return lax.dot_general(a, b, (((0,), (0,)), ((), ())), preferred_element_type=F32)


def _ffn_kernel(x_ref, nw_ref, wg_ref, wu_ref, wd_ref, o_ref, xn_ref, acc_ref):
    xn_ref[...] = _rms_rows(x_ref[...], nw_ref[...]).astype(BF16)
    acc_ref[...] = jnp.zeros_like(acc_ref)
    for f in range(D_FF // FF_CHUNK):
        cols = slice(f * FF_CHUNK, (f + 1) * FF_CHUNK)
        g = jnp.dot(xn_ref[...], wg_ref[:, cols], preferred_element_type=F32)
        u = jnp.dot(xn_ref[...], wu_ref[:, cols], preferred_element_type=F32)
        acc_ref[...] += jnp.dot((_silu(g) * u).astype(BF16), wd_ref[cols, :], preferred_element_type=F32)
    o_ref[...] = x_ref[...] + 0.5 * acc_ref[...]


def _ffn(h2, norm_w, w_gate, w_up, w_down):
    rows = h2.shape[0]
    row_spec = pl.BlockSpec((FFN_ROW_TILE, D_MODEL), lambda i: (i, 0))
    return pl.pallas_call(
        _ffn_kernel,
        grid=(rows // FFN_ROW_TILE,),
        in_specs=[row_spec, _resident((1, D_MODEL)), _resident(w_gate.shape), _resident(w_up.shape),
                  _resident(w_down.shape)],
        out_specs=row_spec,
        out_shape=jax.ShapeDtypeStruct(h2.shape, F32),
        scratch_shapes=[pltpu.VMEM((FFN_ROW_TILE, D_MODEL), BF16), pltpu.VMEM((FFN_ROW_TILE, D_MODEL), F32)],
        compiler_params=_cparams("parallel"),
    )(h2, norm_w.reshape(1, D_MODEL), w_gate.astype(BF16), w_up.astype(BF16), w_down.astype(BF16))


def _norm_proj_kernel(x_ref, nw_ref, w_ref, ws_ref, o_ref, os_ref, xn_ref, *, col_chunk):
    xn_ref[...] = _rms_rows(x_ref[...], nw_ref[...]).astype(BF16)
    for c in range(w_ref.shape[1] // col_chunk):
        cols = slice(c * col_chunk, (c + 1) * col_chunk)
        o_ref[:, cols] = jnp.dot(xn_ref[...], w_ref[:, cols], preferred_element_type=F32).astype(BF16)
    os_ref[...] = jnp.dot(xn_ref[...], ws_ref[...], preferred_element_type=F32)


def _norm_proj(h2, norm_w, w_in, n_main):
    rows = h2.shape[0]
    w_main = w_in[:, :n_main].astype(BF16)
    n_small = w_in.shape[1] - n_main
    w_small = jnp.pad(w_in[:, n_main:], ((0, 0), (0, LANES - n_small))).astype(BF16)
    row_spec = pl.BlockSpec((ROW_TILE, D_MODEL), lambda i: (i, 0))
    return pl.pallas_call(
        functools.partial(_norm_proj_kernel, col_chunk=2 * FF_CHUNK),
        grid=(rows // ROW_TILE,),
        in_specs=[row_spec, _resident((1, D_MODEL)), _resident(w_main.shape), _resident(w_small.shape)],
        out_specs=[pl.BlockSpec((ROW_TILE, n_main), lambda i: (i, 0)),
                   pl.BlockSpec((ROW_TILE, LANES), lambda i: (i, 0))],
        out_shape=[jax.ShapeDtypeStruct((rows, n_main), BF16),
                   jax.ShapeDtypeStruct((rows, LANES), F32)],
        scratch_shapes=[pltpu.VMEM((ROW_TILE, D_MODEL), BF16)],
        compiler_params=_cparams("parallel"),
    )(h2, norm_w.reshape(1, D_MODEL), w_main, w_small)


def _out_proj_kernel(*refs):
    n = (len(refs) - 2) // 2
    h_ref, o_ref = refs[2 * n], refs[2 * n + 1]
    acc = h_ref[...]
    for y_ref, w_ref in zip(refs[:n], refs[n:2 * n]):
        acc = acc + jnp.dot(y_ref[...], w_ref[...], preferred_element_type=F32)
    o_ref[...] = acc


def _out_proj(ys, ws, h2):
    rows = h2.shape[0]
    row_spec = pl.BlockSpec((ROW_TILE, D_MODEL), lambda i: (i, 0))
    ws = [w.astype(BF16) for w in ws]
    return pl.pallas_call(
        _out_proj_kernel,
        grid=(rows // ROW_TILE,),
        in_specs=([pl.BlockSpec((ROW_TILE, y.shape[1]), lambda i: (i, 0)) for y in ys]
                  + [_resident(w.shape) for w in ws] + [row_spec]),
        out_specs=row_spec,
        out_shape=jax.ShapeDtypeStruct(h2.shape, F32),
        compiler_params=_cparams("parallel"),
    )(*ys, *ws, h2)


def _na_bias_table(rpb):
    qc = np.arange(GRID_W)[:, None]
    kc = np.arange(GRID_W)[None, :]
    dc = np.clip(kc - qc, 1 - NA_WIN_W, NA_WIN_W - 1) + (NA_WIN_W - 1)
    win_c0 = np.clip(qc - NA_WIN_W // 2, 0, GRID_W - NA_WIN_W)
    col_ok = (kc >= win_c0) & (kc < win_c0 + NA_WIN_W)
    onehot = (dc[:, :, None] == np.arange(2 * NA_WIN_W - 1)).astype(np.float32)
    toep = jnp.einsum('hdc,qkc->hdqk', rpb.astype(F32), onehot, precision=lax.Precision.HIGHEST)
    toep = jnp.where(col_ok[None, None], toep, -jnp.inf)
    tbl = jnp.stack([toep[:, d0:d0 + NA_WIN_H] for d0 in range(NA_WIN_H)], axis=1)
    tbl = tbl.transpose(0, 1, 3, 2, 4).reshape(NA_HEADS, NA_WIN_H, GRID_W, NA_WIN_H * GRID_W)
    meta = jnp.zeros((NA_HEADS, NA_WIN_H, GRID_W, N_META), F32)
    return jnp.concatenate([tbl, meta], axis=-1)


def _natten_kernel(q_ref, k_ref, v_ref, bias_ref, o_ref, *, n_rows):
    scale = NA_HEAD_DIM ** -0.5
    g0 = PAD + N_META
    n_keys = NA_WIN_H * GRID_W

    pairs = [slice(j * LANES, (j + 1) * LANES) for j in range(NA_HEADS // 2)]

    def attend(jobs):
        low = lax.broadcasted_iota(jnp.int32, (jobs[0][0].shape[0], LANES), 1) < NA_HEAD_DIM
        s = []
        for q, k, _, _ in jobs:
            for c in pairs:
                qp = q[:, c]
                s.append(_nt(jnp.where(low, qp, jnp.zeros_like(qp)), k[:, c]))
                s.append(_nt(jnp.where(low, jnp.zeros_like(qp), qp), k[:, c]))
        p, l = [], []
        for i, (_, _, _, bias) in enumerate(jobs):
            for h in range(NA_HEADS):
                sh = s[i * NA_HEADS + h] * scale
                if bias is not None:
                    sh = sh + bias(h)
                ph = jnp.exp(sh - jnp.max(sh, axis=1, keepdims=True))
                l.append(jnp.sum(ph, axis=1, keepdims=True))
                p.append(ph.astype(BF16))
        o = [jnp.dot(p[i * NA_HEADS + h], v[:, pairs[h // 2]], preferred_element_type=F32)
             for i, (_, _, v, _) in enumerate(jobs) for h in range(NA_HEADS)]
        o = [x / y for x, y in zip(o, l)]
        return [jnp.concatenate([jnp.where(low, o[i * NA_HEADS + 2 * j], o[i * NA_HEADS + 2 * j + 1])
                                 for j in range(NA_HEADS // 2)], axis=1).astype(BF16)
                for i in range(len(jobs))]

    k_meta = k_ref[0, PAD:g0, :]
    v_meta = v_ref[0, PAD:g0, :]
    o_ref[0, :PAD, :] = jnp.zeros((PAD, D_A), BF16)
    o_ref[0, PAD:g0, :] = attend([(q_ref[0, PAD:g0, :], k_meta, v_meta, None)])[0]

    def row_job(r):
        first = jnp.clip(r - NA_WIN_H // 2, 0, n_rows - NA_WIN_H)
        d0 = first - r + (NA_WIN_H - 1)
        q_rows = pl.ds(pl.multiple_of(g0 + r * GRID_W, GRID_W), GRID_W)
        k_rows = pl.ds(pl.multiple_of(g0 + first * GRID_W, GRID_W), n_keys)
        k = jnp.concatenate([k_ref[0, k_rows, :], k_meta], axis=0)
        v = jnp.concatenate([v_ref[0, k_rows, :], v_meta], axis=0)
        return q_rows, (q_ref[0, q_rows, :], k, v, lambda h: bias_ref[h, d0])

    def rows_body(i, carry):
        q_rows, jobs = zip(*[row_job(i * NA_ROWS_PER_STEP + j) for j in range(NA_ROWS_PER_STEP)])
        for rows, o in zip(q_rows, attend(list(jobs))):
            o_ref[0, rows, :] = o
        return carry

    lax.fori_loop(0, n_rows // NA_ROWS_PER_STEP, rows_body, 0)


def _natten(proj, bias_tbl):
    bsz, tp, _ = proj.shape
    n_rows = (tp - PAD - N_META) // GRID_W

    def col_block(j):
        return pl.BlockSpec((1, tp, D_A), lambda b: (b, 0, j))

    return pl.pallas_call(
        functools.partial(_natten_kernel, n_rows=n_rows),
        grid=(bsz,),
        in_specs=[col_block(0), col_block(1), col_block(2), _resident(bias_tbl.shape)],
        out_specs=pl.BlockSpec((1, tp, D_A), lambda b: (b, 0, 0)),
        out_shape=jax.ShapeDtypeStruct((bsz, tp, D_A), BF16),
        compiler_params=_cparams("parallel"),
    )(proj, proj, proj, bias_tbl)


def _fill_conv_src(src_ref, parts, tp):
    width = src_ref.shape[1]
    src_ref[0:8 + PAD, :] = jnp.zeros((8 + PAD, width), F32)
    src_ref[8 + tp:16 + tp, :] = jnp.zeros((8, width), F32)
    off = 0
    for p in parts:
        w = p.shape[2]
        src_ref[8 + PAD:8 + tp, off:off + w] = p[0, PAD:tp, :].astype(F32)
        off += w


def _conv_silu(src_ref, w, b, c):
    x = src_ref[pl.ds(pl.multiple_of(c * CHUNK, CHUNK), CHUNK + 16), :]
    acc = b + w[0:1, :] * x[8 - CONV_HALF:8 - CONV_HALF + CHUNK]
    for j in range(1, CONV_W):
        lo = 8 - CONV_HALF + j
        acc = acc + w[j:j + 1, :] * x[lo:lo + CHUNK]
    return _silu(acc)


def _chunk_rows(c):
    return pl.ds(pl.multiple_of(c * CHUNK, CHUNK), CHUNK)


def _scan_steps(i, n_chunks):
    cs = {}
    for u in range(SCAN_UNROLL):
        cs[u, 0] = i * SCAN_UNROLL + u
        cs[u, 1] = n_chunks - 1 - cs[u, 0]
    return cs, {k: _chunk_rows(c) for k, c in cs.items()}


def _scan_tri(shape, reverse):
    t = lax.broadcasted_iota(jnp.int32, shape, 0)
    s = lax.broadcasted_iota(jnp.int32, shape, 1) % CHUNK
    return (s >= t) if reverse else (s <= t)


def _cumsum_rows(x):
    row = lax.broadcasted_iota(jnp.int32, x.shape, 0)
    sh = 1
    while sh < CHUNK:
        x = x + jnp.where(row >= sh, pltpu.roll(x, sh, axis=0), 0.0)
        sh *= 2
    return x


def _pack3(x):
    hi = x.astype(BF16).astype(F32)
    r1 = x - hi
    mid = r1.astype(BF16).astype(F32)
    return jnp.concatenate([hi + pltpu.roll(mid, HALF, axis=1), r1 - mid], axis=1).astype(BF16)


def _pack2(x):
    hi = x.astype(BF16)
    return jnp.concatenate([hi, (x - hi.astype(F32)).astype(BF16)], axis=1)


def _selector(lane_of_row, terms):
    k = lax.broadcasted_iota(jnp.int32, lane_of_row.shape, 1)
    hit = (k == lane_of_row) | (k == lane_of_row + LANES)
    if terms == 3:
        hit = hit | (k == lane_of_row + HALF)
    return jnp.where(hit, 1.0, 0.0).astype(BF16)


def _mlstm_kernel(q_ref, k_ref, v_ref, og_ref, gates_ref, cwq_ref, cwk_ref, cbq_ref, cbk_ref,
                  gb_ref, nw_ref, o_ref, src_ref, qc_ref, kc_ref, x_ref, lf_ref, g_ref, w_ref, pkw_ref,
                  stat_ref, m_ref, a_ref, wi_ref, em_ref, wv_ref, wo_ref, y_ref, st_ref):
    tp = q_ref.shape[1]
    n_chunks = tp // CHUNK
    head = pl.program_id(1)
    hd = ML_HEAD_DIM

    _fill_conv_src(src_ref, (q_ref, k_ref), tp)
    row = lax.broadcasted_iota(jnp.int32, (tp, LANES), 0)
    st_ref[...] = jnp.zeros_like(st_ref)
    y_ref[...] = jnp.zeros_like(y_ref)

    @pl.when(head == 0)
    def _gates():
        x_all = jnp.where(row < PAD, 0.0, gates_ref[0] + gb_ref[...])
        x_ref[...] = x_all
        lf_ref[...] = _log_sigmoid(x_all)

        glane = lax.broadcasted_iota(jnp.int32, (CHUNK, LANES), 1)

        def loop_g(c, carry):
            rows = _chunk_rows(c)
            lf = lf_ref[rows, :]
            gf = _cumsum_rows(lf)
            xi = pltpu.roll(x_ref[rows, :], ML_HEADS, axis=1)
            tot = gf[CHUNK - 1:CHUNK]
            stats = []
            for d, g in enumerate((gf, tot - gf + lf)):
                w = xi - g
                g_ref[d, rows, :] = g
                w_ref[d, rows, :] = w
                pkw_ref[d, rows, :] = _pack3(jnp.where(glane < HALF, w, 0.0))
                stats += [tot, jnp.max(tot + w, axis=0, keepdims=True)]
            stat_ref[c] = jnp.concatenate(stats + [jnp.zeros((4, LANES), F32)], axis=0)
            return carry

        lax.fori_loop(0, n_chunks, loop_g, 0)

        def loop_m(i, m):
            mf, mb = m
            sf = stat_ref[i]
            nf = jnp.maximum(sf[0:1] + mf, sf[1:2])
            m_ref[0, i] = jnp.concatenate([mf, nf, jnp.zeros((6, LANES), F32)], axis=0)
            cb = n_chunks - 1 - i
            sb = stat_ref[cb]
            nb = jnp.maximum(sb[2:3] + mb, sb[3:4])
            m_ref[1, cb] = jnp.concatenate([mb, nb, jnp.zeros((6, LANES), F32)], axis=0)
            return nf, nb

        zero_row = jnp.zeros((1, LANES), F32)
        lax.fori_loop(0, n_chunks, loop_m, (zero_row, zero_row))

    conv_w = jnp.concatenate([cwq_ref[...], cwk_ref[...]], axis=1)
    conv_b = jnp.concatenate([cbq_ref[...], cbk_ref[...]], axis=1)

    def loop_a(c, carry):
        rows = _chunk_rows(c)
        qk = _conv_silu(src_ref, conv_w, conv_b, c)
        qc_ref[rows, :] = qk[:, :hd].astype(BF16)
        kc_ref[rows, :] = (qk[:, hd:] * (hd ** -0.5)).astype(BF16)
        return carry

    lax.fori_loop(0, n_chunks, loop_a, 0)
    qc_ref[0:PAD, :] = jnp.zeros((PAD, hd), BF16)
    kc_ref[0:PAD, :] = jnp.zeros((PAD, hd), BF16)

    lane = lax.broadcasted_iota(jnp.int32, (CHUNK, LANES), 1)
    lane1 = lax.broadcasted_iota(jnp.int32, (1, LANES), 1)
    vrow = lax.broadcasted_iota(jnp.int32, (CHUNK, hd), 0)

    def pick(x, lane_ids, f_lane):
        return jnp.sum(jnp.where(lane_ids == f_lane, x, 0.0), axis=1, keepdims=True)

    units = [(u, d) for u in range(SCAN_UNROLL) for d in range(2)]
    f_lanes = [2 * ML_HEADS * d + ML_HEADS + head for d in range(2)]
    sels = [_selector(jnp.full((8, 2 * LANES), f, jnp.int32), 3) for f in f_lanes]

    def loop_p(i, carry):
        cs = [i * SCAN_UNROLL + u for u in range(SCAN_UNROLL)]
        rows = [_chunk_rows(c) for c in cs]
        qk = [_nt(qc_ref[r, :], kc_ref[r, :]) for r in rows]
        w_row = {(u, d): _nt(sels[d], pkw_ref[d, rows[u], :]) for u, d in units}
        v1, g_col, w_col = [], {}, {}
        for u in range(SCAN_UNROLL):
            v = jnp.where(vrow + cs[u] * CHUNK < PAD, 0.0, v_ref[0, rows[u], :].astype(F32))
            v1.append(jnp.concatenate([v, jnp.ones((CHUNK, hd), F32)], axis=1))
            for d in range(2):
                g_col[u, d] = pick(g_ref[d, rows[u], :], lane, f_lanes[d])
                w_col[u, d] = pick(w_ref[d, rows[u], :], lane, f_lanes[d])
        sw = {}
        for u, d in units:
            reverse = d == 1
            gc = g_col[u, d]
            dmat = gc + jnp.concatenate([w_row[u, d]] * (CHUNK // 8), axis=0)
            dmat = jnp.where(_scan_tri((CHUNK, CHUNK), reverse), dmat, -jnp.inf)
            mm = m_ref[d, cs[u]]
            m_st = pick(mm[0:1], lane1, f_lanes[d])
            m_new = pick(mm[1:2], lane1, f_lanes[d])
            g_last = gc[0:1] if reverse else gc[CHUNK - 1:CHUNK]
            m_inter = gc + m_st
            m_t = jnp.maximum(m_inter, jnp.max(dmat, axis=1, keepdims=True))
            sw[u, d] = (jnp.exp(dmat - m_t) * qk[u]).astype(BF16)
            wi_ref[d, rows[u], :] = jnp.broadcast_to(jnp.exp(m_inter - m_t), (CHUNK, LANES))
            em_ref[d, rows[u], :] = jnp.broadcast_to(jnp.exp(-m_t), (CHUNK, LANES))
            w_end = jnp.exp(g_last + w_col[u, d] - m_new)
            wv_ref[d, rows[u], :] = (w_end * v1[u]).astype(BF16)
            wo_ref[d, cs[u]] = jnp.broadcast_to(jnp.exp(g_last + m_st - m_new), (8, LANES))
        v1b = [x.astype(BF16) for x in v1]
        intra = {(u, d): jnp.dot(sw[u, d], v1b[u], preferred_element_type=F32) for u, d in units}
        for u, d in units:
            a_ref[d, rows[u], :] = intra[u, d]
        return carry

    lax.fori_loop(0, n_chunks // SCAN_UNROLL, loop_p, 0)

    def loop_s(i, carry):
        cs, rows = _scan_steps(i, n_chunks)
        upd = {k: _tn(kc_ref[rows[k], :], wv_ref[k[1], rows[k], :]) for k in units}
        st = {(0, d): st_ref[d] for d in range(2)}
        inter = {}
        for u in range(SCAN_UNROLL):
            for d in range(2):
                inter[u, d] = jnp.dot(qc_ref[rows[u, d], :], st[u, d].astype(BF16),
                                      preferred_element_type=F32)
                st[u + 1, d] = wo_ref[d, cs[u, d]][0:1, 0:1] * st[u, d] + upd[u, d]
        for u, d in units:
            r = rows[u, d]
            wi = wi_ref[d, r, :]
            tot = a_ref[d, r, :] + jnp.concatenate([wi, wi], axis=1) * inter[u, d]
            y_ref[r, :] += tot[:, :hd] / jnp.maximum(jnp.abs(tot[:, hd:]), em_ref[d, r, :])
        for d in range(2):
            st_ref[d] = st[SCAN_UNROLL, d]
        return carry

    lax.fori_loop(0, n_chunks // SCAN_UNROLL, loop_s, 0)

    y = y_ref[...]
    y = y * lax.rsqrt(jnp.mean(y * y, axis=1, keepdims=True) + RMS_EPS) * nw_ref[...]
    y = y * _sigmoid(og_ref[0].astype(F32))
    o_ref[0] = jnp.where(row < PAD, 0.0, y).astype(BF16)


def _mlstm(proj, gates, conv_w, conv_b, gate_bias, norm_w):
    bsz, tp, _ = proj.shape
    hd = ML_HEAD_DIM
    nc = tp // CHUNK
    base = 3 * D_A // hd

    def col_block(j0):
        return pl.BlockSpec((1, tp, hd), lambda b, h: (b, 0, j0 + h))

    def par_block(nrow, j0):
        return pl.BlockSpec((nrow, hd), lambda b, h: (0, j0 + h))

    gb = jnp.pad(gate_bias.astype(F32), (0, LANES - gate_bias.shape[0])).reshape(1, LANES)
    return pl.pallas_call(
        _mlstm_kernel,
        grid=(bsz, ML_HEADS),
        in_specs=[col_block(base), col_block(base + ML_HEADS), col_block(base + 2 * ML_HEADS),
                  col_block(base + 3 * ML_HEADS),
                  pl.BlockSpec((1, tp, LANES), lambda b, h: (b, 0, 0)),
                  par_block(CONV_W, 0), par_block(CONV_W, ML_HEADS),
                  par_block(1, 0), par_block(1, ML_HEADS),
                  pl.BlockSpec((1, LANES), lambda b, h: (0, 0)),
                  par_block(1, 0)],
        out_specs=pl.BlockSpec((1, tp, hd), lambda b, h: (b, 0, h)),
        out_shape=jax.ShapeDtypeStruct((bsz, tp, D_B), BF16),
        scratch_shapes=[pltpu.VMEM((tp + 16, 2 * hd), F32),
                        pltpu.VMEM((tp, hd), BF16),
                        pltpu.VMEM((tp, hd), BF16),
                        pltpu.VMEM((tp, LANES), F32),
                        pltpu.VMEM((tp, LANES), F32),
                        pltpu.VMEM((2, tp, LANES), F32),
                        pltpu.VMEM((2, tp, LANES), F32),
                        pltpu.VMEM((2, tp, 2 * LANES), BF16),
                        pltpu.VMEM((nc, 8, LANES), F32),
                        pltpu.VMEM((2, nc, 8, LANES), F32),
                        pltpu.VMEM((2, tp, 2 * hd), F32),
                        pltpu.VMEM((2, tp, LANES), F32),
                        pltpu.VMEM((2, tp, LANES), F32),
                        pltpu.VMEM((2, tp, 2 * hd), BF16),
                        pltpu.VMEM((2, nc, 8, LANES), F32),
                        pltpu.VMEM((tp, hd), F32),
                        pltpu.VMEM((2, hd, 2 * hd), F32)],
        compiler_params=_cparams("parallel", "arbitrary"),
    )(proj, proj, proj, proj, gates, conv_w.astype(F32), conv_w.astype(F32),
      conv_b.astype(F32).reshape(1, -1), conv_b.astype(F32).reshape(1, -1), gb,
      norm_w.astype(F32).reshape(1, -1))


def _ssd_kernel(z_ref, x_ref, b_ref, c_ref, dt_ref, cwx_ref, cwb_ref, cwc_ref, cbx_ref, cbb_ref,
                cbc_ref, dtb_ref, alog_ref, dsk_ref, nw_ref, o_ref,
                src_ref, xs_ref, bc_ref, cc_ref, dts_ref, da_ref, sel_ref, rsel_ref,
                pk_ref, cum4_ref, xdt_ref, xw_ref, ecl_ref, y_ref, st_ref):
    tp = x_ref.shape[1]
    n_chunks = tp // CHUNK
    assert n_chunks % SCAN_UNROLL == 0
    grp = pl.program_id(1)
    gw = SSD_GW
    ns = SSD_STATE
    hdim = SSD_HEAD_DIM

    _fill_conv_src(src_ref, (x_ref, b_ref, c_ref), tp)
    st_ref[...] = jnp.zeros_like(st_ref)

    @pl.when(grp == 0)
    def _dt():
        row = lax.broadcasted_iota(jnp.int32, (tp, LANES), 0)
        lane_all = lax.broadcasted_iota(jnp.int32, (tp, LANES), 1)
        dt_all = jnp.where((row < PAD) | (lane_all >= 2 * SSD_HEADS), 0.0,
                           _softplus(dt_ref[0] + dtb_ref[...]))
        dts_ref[...] = dt_all
        da_ref[...] = dt_all * (-jnp.exp(alog_ref[...]))

        def loop_c(c, carry):
            rows = _chunk_rows(c)
            da = da_ref[rows, :]
            cumf = _cumsum_rows(da)
            pk_ref[0, rows, :] = _pack3(cumf)
            pk_ref[1, rows, :] = _pack3(cumf[CHUNK - 1:CHUNK] - cumf + da)
            pk_ref[2, rows, :] = _pack2(dts_ref[rows, :])
            return carry

        lax.fori_loop(0, n_chunks, loop_c, 0)

    col_head = lax.broadcasted_iota(jnp.int32, (gw, 2 * LANES), 0) // hdim
    row_head = lax.broadcasted_iota(jnp.int32, (8 * SSD_HPG, 2 * LANES), 0) // 8
    for d in range(2):
        lane0 = SSD_HEADS * d + SSD_HPG * grp
        sel_ref[d, 0] = _selector(lane0 + col_head, 3)
        sel_ref[d, 1] = _selector(lane0 + col_head, 2)
        rsel_ref[d] = _selector(lane0 + row_head, 3)

    conv_w = jnp.concatenate([cwx_ref[...], cwb_ref[...], cwc_ref[...]], axis=1)
    conv_b = jnp.concatenate([cbx_ref[...], cbb_ref[...], cbc_ref[...]], axis=1)

    def loop_a(c, carry):
        rows = _chunk_rows(c)
        a = _conv_silu(src_ref, conv_w, conv_b, c)
        xs_ref[rows, :] = a[:, :gw]
        bc_ref[rows, :] = a[:, gw:gw + ns].astype(BF16)
        cc_ref[rows, :] = a[:, gw + ns:].astype(BF16)
        return carry

    lax.fori_loop(0, n_chunks, loop_a, 0)
    xs_ref[0:PAD, :] = jnp.zeros((PAD, gw), F32)
    bc_ref[0:PAD, :] = jnp.zeros((PAD, ns), BF16)
    cc_ref[0:PAD, :] = jnp.zeros((PAD, ns), BF16)

    eb_rows = tp // EXPAND_STEPS

    def loop_e(i, carry):
        rows = pl.ds(pl.multiple_of(i * eb_rows, CHUNK), eb_rows)
        cum4 = [_nt(pk_ref[d, rows, :], sel_ref[d, 0]) for d in range(2)]
        dt4 = [_nt(pk_ref[2, rows, :], sel_ref[d, 1]) for d in range(2)]
        xs = xs_ref[rows, :]
        for d in range(2):
            cum4_ref[d, rows, :] = cum4[d]
            xdt_ref[d, rows, :] = xs * dt4[d]
        return carry

    lax.fori_loop(0, EXPAND_STEPS, loop_e, 0)

    blk4 = lax.broadcasted_iota(jnp.int32, (CHUNK, gw), 1) // hdim
    units = [(u, d) for u in range(SCAN_UNROLL) for d in range(2)]

    def loop_p(i, carry):
        cs = [i * SCAN_UNROLL + u for u in range(SCAN_UNROLL)]
        rows = [_chunk_rows(c) for c in cs]
        cb2 = [_nt(cc_ref[r, :], jnp.concatenate([bc_ref[r, :]] * 2, axis=0)) for r in rows]
        cb4 = [jnp.concatenate([x] * (SSD_HPG // 2), axis=1) for x in cb2]
        row32 = {(u, d): _nt(rsel_ref[d], pk_ref[d, rows[u], :]) for u, d in units}
        m4, xdiag = {}, {}
        for u, d in units:
            reverse = d == 1
            cum4 = cum4_ref[d, rows[u], :]
            r32 = row32[u, d]
            row_cum = jnp.concatenate([r32[8 * r:8 * r + 8] for r in range(SSD_HPG)], axis=1)
            seg4 = cum4 - jnp.concatenate([row_cum] * (CHUNK // 8), axis=0)
            decay4 = jnp.where(_scan_tri((CHUNK, gw), reverse), jnp.exp(seg4), 0.0)
            m4[u, d] = (decay4 * cb4[u]).astype(BF16)
            xdt = xdt_ref[d, rows[u], :]
            xdt_b = xdt.astype(BF16)
            xdiag[u, d] = jnp.concatenate([jnp.where(blk4 == r, xdt_b, jnp.zeros_like(xdt_b))
                                           for r in range(SSD_HPG)], axis=0)
            cum_last = cum4[0:1] if reverse else cum4[CHUNK - 1:CHUNK]
            xw_ref[d, rows[u], :] = (jnp.exp(cum_last - cum4) * xdt).astype(BF16)
            ecl_ref[d, cs[u]] = jnp.broadcast_to(jnp.exp(cum_last), (8, gw))
        intra = {k: jnp.dot(m4[k], xdiag[k], preferred_element_type=F32) for k in units}
        for u in range(SCAN_UNROLL):
            y_ref[rows[u], :] = dsk_ref[...] * xs_ref[rows[u], :] + intra[u, 0] + intra[u, 1]
        return carry

    lax.fori_loop(0, n_chunks // SCAN_UNROLL, loop_p, 0)

    def loop_s(i, carry):
        cs, rows = _scan_steps(i, n_chunks)
        upd = {k: _tn(bc_ref[rows[k], :], xw_ref[k[1], rows[k], :]) for k in units}
        st = {(0, d): st_ref[d] for d in range(2)}
        inter = {}
        for u in range(SCAN_UNROLL):
            for d in range(2):
                inter[u, d] = jnp.dot(cc_ref[rows[u, d], :], st[u, d].astype(BF16),
                                      preferred_element_type=F32)
                st[u + 1, d] = ecl_ref[d, cs[u, d]][0:1] * st[u, d] + upd[u, d]
        for u, d in units:
            y_ref[rows[u, d], :] += jnp.exp(cum4_ref[d, rows[u, d], :]) * inter[u, d]
        for d in range(2):
            st_ref[d] = st[SCAN_UNROLL, d]
        return carry

    lax.fori_loop(0, n_chunks // SCAN_UNROLL, loop_s, 0)

    rowg = lax.broadcasted_iota(jnp.int32, (tp, gw), 0)
    y = y_ref[...] * _silu(z_ref[0].astype(F32))
    y = y * lax.rsqrt(jnp.mean(y * y, axis=1, keepdims=True) + RMS_EPS) * nw_ref[...]
    o_ref[0] = jnp.where(rowg < PAD, 0.0, y).astype(BF16)


def _ssd(proj, dt_raw, conv_w, conv_b, dt_bias, a_log, d_skip, norm_w):
    bsz, tp, _ = proj.shape
    gw, ns = SSD_GW, SSD_STATE
    nc = tp // CHUNK
    xb, bb, cb = D_INNER // gw, 2 * D_INNER // ns, 2 * D_INNER // ns + SSD_GROUPS

    def col_block(width, j0):
        return pl.BlockSpec((1, tp, width), lambda b, g: (b, 0, j0 + g))

    def par_block(nrow, width, j0):
        return pl.BlockSpec((nrow, width), lambda b, g: (0, j0 + g))

    def lane_row(v):
        v = v.astype(F32).reshape(-1)
        return jnp.pad(v, (0, LANES - v.shape[0])).reshape(1, LANES)

    cw = conv_w.astype(F32)
    cbias = conv_b.astype(F32).reshape(1, -1)
    xb_c, bb_c, cb_c = 0, D_INNER // ns, D_INNER // ns + SSD_GROUPS
    dsk = jnp.repeat(d_skip.astype(F32), SSD_HEAD_DIM).reshape(1, D_INNER)
    full_row = pl.BlockSpec((1, LANES), lambda b, g: (0, 0))
    return pl.pallas_call(
        _ssd_kernel,
        grid=(bsz, SSD_GROUPS),
        in_specs=[col_block(gw, 0), col_block(gw, xb), col_block(ns, bb), col_block(ns, cb),
                  pl.BlockSpec((1, tp, LANES), lambda b, g: (b, 0, 0)),
                  par_block(CONV_W, gw, xb_c), par_block(CONV_W, ns, bb_c), par_block(CONV_W, ns, cb_c),
                  par_block(1, gw, xb_c), par_block(1, ns, bb_c), par_block(1, ns, cb_c),
                  full_row, full_row, par_block(1, gw, 0), par_block(1, gw, 0)],
        out_specs=pl.BlockSpec((1, tp, gw), lambda b, g: (b, 0, g)),
        out_shape=jax.ShapeDtypeStruct((bsz, tp, D_INNER), BF16),
        scratch_shapes=[pltpu.VMEM((tp + 16, gw + 2 * ns), F32),
                        pltpu.VMEM((tp, gw), F32),
                        pltpu.VMEM((tp, ns), BF16),
                        pltpu.VMEM((tp, ns), BF16),
                        pltpu.VMEM((tp, LANES), F32),
                        pltpu.VMEM((tp, LANES), F32),
                        pltpu.VMEM((2, 2, gw, 2 * LANES), BF16),
                        pltpu.VMEM((2, 8 * SSD_HPG, 2 * LANES), BF16),
                        pltpu.VMEM((3, tp, 2 * LANES), BF16),
                        pltpu.VMEM((2, tp, gw), F32),
                        pltpu.VMEM((2, tp, gw), F32),
                        pltpu.VMEM((2, tp, gw), BF16),
                        pltpu.VMEM((2, nc, 8, gw), F32),
                        pltpu.VMEM((tp, gw), F32),
                        pltpu.VMEM((2, ns, gw), F32)],
        compiler_params=_cparams("parallel", "arbitrary"),
    )(proj, proj, proj, proj, dt_raw, cw, cw, cw, cbias, cbias, cbias,
      lane_row(dt_bias), lane_row(a_log), dsk, norm_w.astype(F32).reshape(1, -1))


def _final_kernel(x_ref, nw_ref, o_ref):
    seq = o_ref.shape[1]
    step = 256
    for r0 in range(0, seq, step):
        x = x_ref[0, PAD + N_META + r0:PAD + N_META + r0 + step, :]
        o_ref[0, r0:r0 + step, :] = _rms_rows(x, nw_ref[...])


def _final_norm(h3, norm_w, seq):
    bsz, tp, _ = h3.shape
    return pl.pallas_call(
        _final_kernel,
        grid=(bsz,),
        in_specs=[pl.BlockSpec((1, tp, D_MODEL), lambda b: (b, 0, 0)), _resident((1, D_MODEL))],
        out_specs=pl.BlockSpec((1, seq, D_MODEL), lambda b: (b, 0, 0)),
        out_shape=jax.ShapeDtypeStruct((bsz, seq, D_MODEL), F32),
        compiler_params=_cparams("parallel"),
    )(h3, norm_w.reshape(1, D_MODEL))


def _even_mixer(h2, bsz, tp, norm_w, w_in, rpb, conv_w, conv_b, gate_bias, ml_norm_w, w_out):
    proj, gates = _norm_proj(h2, norm_w, w_in, EV_MAIN)
    proj = proj.reshape(bsz, tp, EV_MAIN)
    y_a = _natten(proj, _na_bias_table(rpb))
    y_b = _mlstm(proj, gates.reshape(bsz, tp, LANES), conv_w, conv_b, gate_bias, ml_norm_w)
    return _out_proj([y_a.reshape(bsz * tp, D_A), y_b.reshape(bsz * tp, D_B)],
                     [w_out[:D_A], w_out[D_A:]], h2)


def _odd_mixer(h2, bsz, tp, norm_w, w_in, conv_w, conv_b, dt_bias, a_log, d_skip, ssd_norm_w, w_out):
    proj, dt_raw = _norm_proj(h2, norm_w, w_in, OD_MAIN)
    y = _ssd(proj.reshape(bsz, tp, OD_MAIN), dt_raw.reshape(bsz, tp, LANES), conv_w, conv_b,
             dt_bias, a_log, d_skip, ssd_norm_w)
    return _out_proj([y.reshape(bsz * tp, D_INNER)], [w_out], h2)


def kernel(x, meta_tokens, ffn1_norm, ffn1_gate, ffn1_up, ffn1_down, mix_norm, ffn2_norm, ffn2_gate, ffn2_up, ffn2_down, ev_w_in, na_rpb, ml_conv_w, ml_conv_b, ml_gate_bias, ml_norm_w, ev_w_out, od_w_in, ssd_conv_w, ssd_conv_b, ssd_dt_bias, ssd_a_log, ssd_d, ssd_norm_w, od_w_out, final_norm):
    bsz, seq, _ = x.shape
    tp = PAD + N_META + seq
    assert x.shape[2] == D_MODEL and seq % GRID_W == 0 and tp % ROW_TILE == 0
    assert seq // GRID_W >= NA_WIN_H
    head = jnp.concatenate([jnp.zeros((PAD, D_MODEL), F32), meta_tokens.astype(F32)], axis=0)
    h = jnp.concatenate([jnp.broadcast_to(head[None], (bsz, PAD + N_META, D_MODEL)), x], axis=1)
    h2 = h.reshape(bsz * tp, D_MODEL)
    for layer in range(ffn1_norm.shape[0]):
        h2 = _ffn(h2, ffn1_norm[layer], ffn1_gate[layer], ffn1_up[layer], ffn1_down[layer])
        i = layer // 2
        if layer % 2 == 0:
            h2 = _even_mixer(h2, bsz, tp, mix_norm[layer], ev_w_in[i], na_rpb[i], ml_conv_w[i],
                             ml_conv_b[i], ml_gate_bias[i], ml_norm_w[i], ev_w_out[i])
        else:
            h2 = _odd_mixer(h2, bsz, tp, mix_norm[layer], od_w_in[i], ssd_conv_w[i], ssd_conv_b[i],
                            ssd_dt_bias[i], ssd_a_log[i], ssd_d[i], ssd_norm_w[i], od_w_out[i])
        h2 = _ffn(h2, ffn2_norm[layer], ffn2_gate[layer], ffn2_up[layer], ffn2_down[layer])
    return _final_norm(h2.reshape(bsz, tp, D_MODEL), final_norm, seq)
```

```python
import functools

import jax
import jax.numpy as jnp
import numpy as np
from jax import lax
from jax.experimental import pallas as pl
from jax.experimental.pallas import tpu as pltpu

F32 = jnp.float32
BF16 = jnp.bfloat16

D_MODEL = 1024
N_META = 16
GRID_W = 64
CHUNK = 64
PAD = CHUNK - N_META
CONV_W = 5
CONV_HALF = (CONV_W - 1) // 2
D_FF = 2816
RMS_EPS = 1e-6

NA_HEADS = 8
NA_HEAD_DIM = 64
NA_WIN_H = 8
NA_WIN_W = 16
ML_HEADS = 4
ML_HEAD_DIM = 128
D_A = NA_HEADS * NA_HEAD_DIM
D_B = ML_HEADS * ML_HEAD_DIM
EV_MAIN = 3 * D_A + 4 * D_B

SSD_HEAD_DIM = 64
SSD_HEADS = 32
SSD_GROUPS = 8
SSD_HPG = SSD_HEADS // SSD_GROUPS
SSD_STATE = 128
D_INNER = SSD_HEADS * SSD_HEAD_DIM
SSD_GW = SSD_HPG * SSD_HEAD_DIM
OD_MAIN = 2 * D_INNER + 2 * SSD_GROUPS * SSD_STATE

LANES = 128
HALF = LANES // 2
FF_CHUNK = 256
ROW_TILE = 528
FFN_ROW_TILE = 1056
NA_ROWS_PER_STEP = 2
SCAN_UNROLL = 33
EXPAND_STEPS = 3
VMEM_LIMIT = 56 * 1024 * 1024


def _cparams(*sem):
    return pltpu.CompilerParams(dimension_semantics=sem, vmem_limit_bytes=VMEM_LIMIT)


def _resident(shape):
    nd = len(shape)
    return pl.BlockSpec(shape, lambda *_: (0,) * nd, pipeline_mode=pl.Buffered(1))


def _rms_rows(x, w_row):
    ms = jnp.mean(x * x, axis=-1, keepdims=True)
    return x * lax.rsqrt(ms + RMS_EPS) * w_row


def _sigmoid(x):
    return 1.0 / (1.0 + jnp.exp(-x))


def _silu(x):
    return x * _sigmoid(x)


def _softplus(x):
    return jnp.maximum(x, 0.0) + jnp.log(1.0 + jnp.exp(-jnp.abs(x)))


def _log_sigmoid(x):
    return -_softplus(-x)


def _nt(a, b):
    return lax.dot_general(a, b, (((1,), (1,)), ((), ())), preferred_element_type=F32)


def _tn(a, b):
    return lax.dot_general(a, b, (((0,), (0,)), ((), ())), preferred_element_type=F32)


def _ffn_kernel(x_ref, nw_ref, wg_ref, wu_ref, wd_ref, o_ref, xn_ref, acc_ref):
    xn_ref[...] = _rms_rows(x_ref[...], nw_ref[...]).astype(BF16)
    acc_ref[...] = jnp.zeros_like(acc_ref)
    for f in range(D_FF // FF_CHUNK):
        cols = slice(f * FF_CHUNK, (f + 1) * FF_CHUNK)
        g = jnp.dot(xn_ref[...], wg_ref[:, cols], preferred_element_type=F32)
        u = jnp.dot(xn_ref[...], wu_ref[:, cols], preferred_element_type=F32)
        acc_ref[...] += jnp.dot((_silu(g) * u).astype(BF16), wd_ref[cols, :], preferred_element_type=F32)
    o_ref[...] = x_ref[...] + 0.5 * acc_ref[...]


def _ffn(h2, norm_w, w_gate, w_up, w_down):
    rows = h2.shape[0]
    row_spec = pl.BlockSpec((FFN_ROW_TILE, D_MODEL), lambda i: (i, 0))
    return pl.pallas_call(
        _ffn_kernel,
        grid=(rows // FFN_ROW_TILE,),
        in_specs=[row_spec, _resident((1, D_MODEL)), _resident(w_gate.shape), _resident(w_up.shape),
                  _resident(w_down.shape)],
        out_specs=row_spec,
        out_shape=jax.ShapeDtypeStruct(h2.shape, F32),
        scratch_shapes=[pltpu.VMEM((FFN_ROW_TILE, D_MODEL), BF16), pltpu.VMEM((FFN_ROW_TILE, D_MODEL), F32)],
        compiler_params=_cparams("parallel"),
    )(h2, norm_w.reshape(1, D_MODEL), w_gate.astype(BF16), w_up.astype(BF16), w_down.astype(BF16))


def _norm_proj_kernel(x_ref, nw_ref, w_ref, ws_ref, o_ref, os_ref, xn_ref, *, col_chunk):
    xn_ref[...] = _rms_rows(x_ref[...], nw_ref[...]).astype(BF16)
    for c in range(w_ref.shape[1] // col_chunk):
        cols = slice(c * col_chunk, (c + 1) * col_chunk)
        o_ref[:, cols] = jnp.dot(xn_ref[...], w_ref[:, cols], preferred_element_type=F32).astype(BF16)
    os_ref[...] = jnp.dot(xn_ref[...], ws_ref[...], preferred_element_type=F32)


def _norm_proj(h2, norm_w, w_in, n_main):
    rows = h2.shape[0]
    w_main = w_in[:, :n_main].astype(BF16)
    n_small = w_in.shape[1] - n_main
    w_small = jnp.pad(w_in[:, n_main:], ((0, 0), (0, LANES - n_small))).astype(BF16)
    row_spec = pl.BlockSpec((ROW_TILE, D_MODEL), lambda i: (i, 0))
    return pl.pallas_call(
        functools.partial(_norm_proj_kernel, col_chunk=2 * FF_CHUNK),
        grid=(rows // ROW_TILE,),
        in_specs=[row_spec, _resident((1, D_MODEL)), _resident(w_main.shape), _resident(w_small.shape)],
        out_specs=[pl.BlockSpec((ROW_TILE, n_main), lambda i: (i, 0)),
                   pl.BlockSpec((ROW_TILE, LANES), lambda i: (i, 0))],
        out_shape=[jax.ShapeDtypeStruct((rows, n_main), BF16),
                   jax.ShapeDtypeStruct((rows, LANES), F32)],
        scratch_shapes=[pltpu.VMEM((ROW_TILE, D_MODEL), BF16)],
        compiler_params=_cparams("parallel"),
    )(h2, norm_w.reshape(1, D_MODEL), w_main, w_small)


def _out_proj_kernel(*refs):
    n = (len(refs) - 2) // 2
    h_ref, o_ref = refs[2 * n], refs[2 * n + 1]
    acc = h_ref[...]
    for y_ref, w_ref in zip(refs[:n], refs[n:2 * n]):
        acc = acc + jnp.dot(y_ref[...], w_ref[...], preferred_element_type=F32)
    o_ref[...] = acc


def _out_proj(ys, ws, h2):
    rows = h2.shape[0]
    row_spec = pl.BlockSpec((ROW_TILE, D_MODEL), lambda i: (i, 0))
    ws = [w.astype(BF16) for w in ws]
    return pl.pallas_call(
        _out_proj_kernel,
        grid=(rows // ROW_TILE,),
        in_specs=([pl.BlockSpec((ROW_TILE, y.shape[1]), lambda i: (i, 0)) for y in ys]
                  + [_resident(w.shape) for w in ws] + [row_spec]),
        out_specs=row_spec,
        out_shape=jax.ShapeDtypeStruct(h2.shape, F32),
        compiler_params=_cparams("parallel"),
    )(*ys, *ws, h2)


def _na_bias_table(rpb):
    qc = np.arange(GRID_W)[:, None]
    kc = np.arange(GRID_W)[None, :]
    dc = np.clip(kc - qc, 1 - NA_WIN_W, NA_WIN_W - 1) + (NA_WIN_W - 1)
    win_c0 = np.clip(qc - NA_WIN_W // 2, 0, GRID_W - NA_WIN_W)
    col_ok = (kc >= win_c0) & (kc < win_c0 + NA_WIN_W)
    onehot = (dc[:, :, None] == np.arange(2 * NA_WIN_W - 1)).astype(np.float32)
    toep = jnp.einsum('hdc,qkc->hdqk', rpb.astype(F32), onehot, precision=lax.Precision.HIGHEST)
    toep = jnp.where(col_ok[None, None], toep, -jnp.inf)
    tbl = jnp.stack([toep[:, d0:d0 + NA_WIN_H] for d0 in range(NA_WIN_H)], axis=1)
    tbl = tbl.transpose(0, 1, 3, 2, 4).reshape(NA_HEADS, NA_WIN_H, GRID_W, NA_WIN_H * GRID_W)
    meta = jnp.zeros((NA_HEADS, NA_WIN_H, GRID_W, N_META), F32)
    return jnp.concatenate([tbl, meta], axis=-1)


def _natten_kernel(q_ref, k_ref, v_ref, bias_ref, o_ref, *, n_rows):
    scale = NA_HEAD_DIM ** -0.5
    g0 = PAD + N_META
    n_keys = NA_WIN_H * GRID_W

    pairs = [slice(j * LANES, (j + 1) * LANES) for j in range(NA_HEADS // 2)]

    def attend(jobs):
        low = lax.broadcasted_iota(jnp.int32, (jobs[0][0].shape[0], LANES), 1) < NA_HEAD_DIM
        s = []
        for q, k, _, _ in jobs:
            for c in pairs:
                qp = q[:, c]
                s.append(_nt(jnp.where(low, qp, jnp.zeros_like(qp)), k[:, c]))
                s.append(_nt(jnp.where(low, jnp.zeros_like(qp), qp), k[:, c]))
        p, l = [], []
        for i, (_, _, _, bias) in enumerate(jobs):
            for h in range(NA_HEADS):
                sh = s[i * NA_HEADS + h] * scale
                if bias is not None:
                    sh = sh + bias(h)
                ph = jnp.exp(sh - jnp.max(sh, axis=1, keepdims=True))
                l.append(jnp.sum(ph, axis=1, keepdims=True))
                p.append(ph.astype(BF16))
        o = [jnp.dot(p[i * NA_HEADS + h], v[:, pairs[h // 2]], preferred_element_type=F32)
             for i, (_, _, v, _) in enumerate(jobs) for h in range(NA_HEADS)]
        o = [x / y for x, y in zip(o, l)]
        return [jnp.concatenate([jnp.where(low, o[i * NA_HEADS + 2 * j], o[i * NA_HEADS + 2 * j + 1])
                                 for j in range(NA_HEADS // 2)], axis=1).astype(BF16)
                for i in range(len(jobs))]

    k_meta = k_ref[0, PAD:g0, :]
    v_meta = v_ref[0, PAD:g0, :]
    o_ref[0, :PAD, :] = jnp.zeros((PAD, D_A), BF16)
    o_ref[0, PAD:g0, :] = attend([(q_ref[0, PAD:g0, :], k_meta, v_meta, None)])[0]

    def row_job(r):
        first = jnp.clip(r - NA_WIN_H // 2, 0, n_rows - NA_WIN_H)
        d0 = first - r + (NA_WIN_H - 1)
        q_rows = pl.ds(pl.multiple_of(g0 + r * GRID_W, GRID_W), GRID_W)
        k_rows = pl.ds(pl.multiple_of(g0 + first * GRID_W, GRID_W), n_keys)
        k = jnp.concatenate([k_ref[0, k_rows, :], k_meta], axis=0)
        v = jnp.concatenate([v_ref[0, k_rows, :], v_meta], axis=0)
        return q_rows, (q_ref[0, q_rows, :], k, v, lambda h: bias_ref[h, d0])

    def rows_body(i, carry):
        q_rows, jobs = zip(*[row_job(i * NA_ROWS_PER_STEP + j) for j in range(NA_ROWS_PER_STEP)])
        for rows, o in zip(q_rows, attend(list(jobs))):
            o_ref[0, rows, :] = o
        return carry

    lax.fori_loop(0, n_rows // NA_ROWS_PER_STEP, rows_body, 0)


def _natten(proj, bias_tbl):
    bsz, tp, _ = proj.shape
    n_rows = (tp - PAD - N_META) // GRID_W

    def col_block(j):
        return pl.BlockSpec((1, tp, D_A), lambda b: (b, 0, j))

    return pl.pallas_call(
        functools.partial(_natten_kernel, n_rows=n_rows),
        grid=(bsz,),
        in_specs=[col_block(0), col_block(1), col_block(2), _resident(bias_tbl.shape)],
        out_specs=pl.BlockSpec((1, tp, D_A), lambda b: (b, 0, 0)),
        out_shape=jax.ShapeDtypeStruct((bsz, tp, D_A), BF16),
        compiler_params=_cparams("parallel"),
    )(proj, proj, proj, bias_tbl)


def _fill_conv_src(src_ref, parts, tp):
    width = src_ref.shape[1]
    src_ref[0:8 + PAD, :] = jnp.zeros((8 + PAD, width), F32)
    src_ref[8 + tp:16 + tp, :] = jnp.zeros((8, width), F32)
    off = 0
    for p in parts:
        w = p.shape[2]
        src_ref[8 + PAD:8 + tp, off:off + w] = p[0, PAD:tp, :].astype(F32)
        off += w


def _conv_silu(src_ref, w, b, c):
    x = src_ref[pl.ds(pl.multiple_of(c * CHUNK, CHUNK), CHUNK + 16), :]
    acc = b + w[0:1, :] * x[8 - CONV_HALF:8 - CONV_HALF + CHUNK]
    for j in range(1, CONV_W):
        lo = 8 - CONV_HALF + j
        acc = acc + w[j:j + 1, :] * x[lo:lo + CHUNK]
    return _silu(acc)


def _chunk_rows(c):
    return pl.ds(pl.multiple_of(c * CHUNK, CHUNK), CHUNK)


def _scan_steps(i, n_chunks):
    cs = {}
    for u in range(SCAN_UNROLL):
        cs[u, 0] = i * SCAN_UNROLL + u
        cs[u, 1] = n_chunks - 1 - cs[u, 0]
    return cs, {k: _chunk_rows(c) for k, c in cs.items()}


def _scan_tri(shape, reverse):
    t = lax.broadcasted_iota(jnp.int32, shape, 0)
    s = lax.broadcasted_iota(jnp.int32, shape, 1) % CHUNK
    return (s >= t) if reverse else (s <= t)


def _cumsum_rows(x):
    row = lax.broadcasted_iota(jnp.int32, x.shape, 0)
    sh = 1
    while sh < CHUNK:
        x = x + jnp.where(row >= sh, pltpu.roll(x, sh, axis=0), 0.0)
        sh *= 2
    return x


def _pack3(x):
    hi = x.astype(BF16).astype(F32)
    r1 = x - hi
    mid = r1.astype(BF16).astype(F32)
    return jnp.concatenate([hi + pltpu.roll(mid, HALF, axis=1), r1 - mid], axis=1).astype(BF16)


def _pack2(x):
    hi = x.astype(BF16)
    return jnp.concatenate([hi, (x - hi.astype(F32)).astype(BF16)], axis=1)


def _selector(lane_of_row, terms):
    k = lax.broadcasted_iota(jnp.int32, lane_of_row.shape, 1)
    hit = (k == lane_of_row) | (k == lane_of_row + LANES)
    if terms == 3:
        hit = hit | (k == lane_of_row + HALF)
    return jnp.where(hit, 1.0, 0.0).astype(BF16)


def _mlstm_kernel(q_ref, k_ref, v_ref, og_ref, gates_ref, cwq_ref, cwk_ref, cbq_ref, cbk_ref,
                  gb_ref, nw_ref, o_ref, src_ref, qc_ref, kc_ref, x_ref, lf_ref, g_ref, w_ref, pkw_ref,
                  stat_ref, m_ref, a_ref, wi_ref, em_ref, wv_ref, wo_ref, y_ref, st_ref):
    tp = q_ref.shape[1]
    n_chunks = tp // CHUNK
    head = pl.program_id(1)
    hd = ML_HEAD_DIM

    _fill_conv_src(src_ref, (q_ref, k_ref), tp)
    row = lax.broadcasted_iota(jnp.int32, (tp, LANES), 0)
    st_ref[...] = jnp.zeros_like(st_ref)
    y_ref[...] = jnp.zeros_like(y_ref)

    @pl.when(head == 0)
    def _gates():
        x_all = jnp.where(row < PAD, 0.0, gates_ref[0] + gb_ref[...])
        x_ref[...] = x_all
        lf_ref[...] = _log_sigmoid(x_all)

        glane = lax.broadcasted_iota(jnp.int32, (CHUNK, LANES), 1)

        def loop_g(c, carry):
            rows = _chunk_rows(c)
            lf = lf_ref[rows, :]
            gf = _cumsum_rows(lf)
            xi = pltpu.roll(x_ref[rows, :], ML_HEADS, axis=1)
            tot = gf[CHUNK - 1:CHUNK]
            stats = []
            for d, g in enumerate((gf, tot - gf + lf)):
                w = xi - g
                g_ref[d, rows, :] = g
                w_ref[d, rows, :] = w
                pkw_ref[d, rows, :] = _pack3(jnp.where(glane < HALF, w, 0.0))
                stats += [tot, jnp.max(tot + w, axis=0, keepdims=True)]
            stat_ref[c] = jnp.concatenate(stats + [jnp.zeros((4, LANES), F32)], axis=0)
            return carry

        lax.fori_loop(0, n_chunks, loop_g, 0)

        def loop_m(i, m):
            mf, mb = m
            sf = stat_ref[i]
            nf = jnp.maximum(sf[0:1] + mf, sf[1:2])
            m_ref[0, i] = jnp.concatenate([mf, nf, jnp.zeros((6, LANES), F32)], axis=0)
            cb = n_chunks - 1 - i
            sb = stat_ref[cb]
            nb = jnp.maximum(sb[2:3] + mb, sb[3:4])
            m_ref[1, cb] = jnp.concatenate([mb, nb, jnp.zeros((6, LANES), F32)], axis=0)
            return nf, nb

        zero_row = jnp.zeros((1, LANES), F32)
        lax.fori_loop(0, n_chunks, loop_m, (zero_row, zero_row))

    conv_w = jnp.concatenate([cwq_ref[...], cwk_ref[...]], axis=1)
    conv_b = jnp.concatenate([cbq_ref[...], cbk_ref[...]], axis=1)

    def loop_a(c, carry):
        rows = _chunk_rows(c)
        qk = _conv_silu(src_ref, conv_w, conv_b, c)
        qc_ref[rows, :] = qk[:, :hd].astype(BF16)
        kc_ref[rows, :] = (qk[:, hd:] * (hd ** -0.5)).astype(BF16)
        return carry

    lax.fori_loop(0, n_chunks, loop_a, 0)
    qc_ref[0:PAD, :] = jnp.zeros((PAD, hd), BF16)
    kc_ref[0:PAD, :] = jnp.zeros((PAD, hd), BF16)

    lane = lax.broadcasted_iota(jnp.int32, (CHUNK, LANES), 1)
    lane1 = lax.broadcasted_iota(jnp.int32, (1, LANES), 1)
    vrow = lax.broadcasted_iota(jnp.int32, (CHUNK, hd), 0)

    def pick(x, lane_ids, f_lane):
        return jnp.sum(jnp.where(lane_ids == f_lane, x, 0.0), axis=1, keepdims=True)

    units = [(u, d) for u in range(SCAN_UNROLL) for d in range(2)]
    f_lanes = [2 * ML_HEADS * d + ML_HEADS + head for d in range(2)]
    sels = [_selector(jnp.full((8, 2 * LANES), f, jnp.int32), 3) for f in f_lanes]

    def loop_p(i, carry):
        cs = [i * SCAN_UNROLL + u for u in range(SCAN_UNROLL)]
        rows = [_chunk_rows(c) for c in cs]
        qk = [_nt(qc_ref[r, :], kc_ref[r, :]) for r in rows]
        w_row = {(u, d): _nt(sels[d], pkw_ref[d, rows[u], :]) for u, d in units}
        v1, g_col, w_col = [], {}, {}
        for u in range(SCAN_UNROLL):
            v = jnp.where(vrow + cs[u] * CHUNK < PAD, 0.0, v_ref[0, rows[u], :].astype(F32))
            v1.append(jnp.concatenate([v, jnp.ones((CHUNK, hd), F32)], axis=1))
            for d in range(2):
                g_col[u, d] = pick(g_ref[d, rows[u], :], lane, f_lanes[d])
                w_col[u, d] = pick(w_ref[d, rows[u], :], lane, f_lanes[d])
        sw = {}
        for u, d in units:
            reverse = d == 1
            gc = g_col[u, d]
            dmat = gc + jnp.concatenate([w_row[u, d]] * (CHUNK // 8), axis=0)
            dmat = jnp.where(_scan_tri((CHUNK, CHUNK), reverse), dmat, -jnp.inf)
            mm = m_ref[d, cs[u]]
            m_st = pick(mm[0:1], lane1, f_lanes[d])
            m_new = pick(mm[1:2], lane1, f_lanes[d])
            g_last = gc[0:1] if reverse else gc[CHUNK - 1:CHUNK]
            m_inter = gc + m_st
            m_t = jnp.maximum(m_inter, jnp.max(dmat, axis=1, keepdims=True))
            sw[u, d] = (jnp.exp(dmat - m_t) * qk[u]).astype(BF16)
            wi_ref[d, rows[u], :] = jnp.broadcast_to(jnp.exp(m_inter - m_t), (CHUNK, LANES))
            em_ref[d, rows[u], :] = jnp.broadcast_to(jnp.exp(-m_t), (CHUNK, LANES))
            w_end = jnp.exp(g_last + w_col[u, d] - m_new)
            wv_ref[d, rows[u], :] = (w_end * v1[u]).astype(BF16)
            wo_ref[d, cs[u]] = jnp.broadcast_to(jnp.exp(g_last + m_st - m_new), (8, LANES))
        v1b = [x.astype(BF16) for x in v1]
        intra = {(u, d): jnp.dot(sw[u, d], v1b[u], preferred_element_type=F32) for u, d in units}
        for u, d in units:
            a_ref[d, rows[u], :] = intra[u, d]
        return carry

    lax.fori_loop(0, n_chunks // SCAN_UNROLL, loop_p, 0)

    def loop_s(i, carry):
        cs, rows = _scan_steps(i, n_chunks)
        upd = {k: _tn(kc_ref[rows[k], :], wv_ref[k[1], rows[k], :]) for k in units}
        st = {(0, d): st_ref[d] for d in range(2)}
        inter = {}
        for u in range(SCAN_UNROLL):
            for d in range(2):
                inter[u, d] = jnp.dot(qc_ref[rows[u, d], :], st[u, d].astype(BF16),
                                      preferred_element_type=F32)
                st[u + 1, d] = wo_ref[d, cs[u, d]][0:1, 0:1] * st[u, d] + upd[u, d]
        for u, d in units:
            r = rows[u, d]
            wi = wi_ref[d, r, :]
            tot = a_ref[d, r, :] + jnp.concatenate([wi, wi], axis=1) * inter[u, d]
            y_ref[r, :] += tot[:, :hd] / jnp.maximum(jnp.abs(tot[:, hd:]), em_ref[d, r, :])
        for d in range(2):
            st_ref[d] = st[SCAN_UNROLL, d]
        return carry

    lax.fori_loop(0, n_chunks // SCAN_UNROLL, loop_s, 0)

    y = y_ref[...]
    y = y * lax.rsqrt(jnp.mean(y * y, axis=1, keepdims=True) + RMS_EPS) * nw_ref[...]
    y = y * _sigmoid(og_ref[0].astype(F32))
    o_ref[0] = jnp.where(row < PAD, 0.0, y).astype(BF16)


def _mlstm(proj, gates, conv_w, conv_b, gate_bias, norm_w):
    bsz, tp, _ = proj.shape
    hd = ML_HEAD_DIM
    nc = tp // CHUNK
    base = 3 * D_A // hd

    def col_block(j0):
        return pl.BlockSpec((1, tp, hd), lambda b, h: (b, 0, j0 + h))

    def par_block(nrow, j0):
        return pl.BlockSpec((nrow, hd), lambda b, h: (0, j0 + h))

    gb = jnp.pad(gate_bias.astype(F32), (0, LANES - gate_bias.shape[0])).reshape(1, LANES)
    return pl.pallas_call(
        _mlstm_kernel,
        grid=(bsz, ML_HEADS),
        in_specs=[col_block(base), col_block(base + ML_HEADS), col_block(base + 2 * ML_HEADS),
                  col_block(base + 3 * ML_HEADS),
                  pl.BlockSpec((1, tp, LANES), lambda b, h: (b, 0, 0)),
                  par_block(CONV_W, 0), par_block(CONV_W, ML_HEADS),
                  par_block(1, 0), par_block(1, ML_HEADS),
                  pl.BlockSpec((1, LANES), lambda b, h: (0, 0)),
                  par_block(1, 0)],
        out_specs=pl.BlockSpec((1, tp, hd), lambda b, h: (b, 0, h)),
        out_shape=jax.ShapeDtypeStruct((bsz, tp, D_B), BF16),
        scratch_shapes=[pltpu.VMEM((tp + 16, 2 * hd), F32),
                        pltpu.VMEM((tp, hd), BF16),
                        pltpu.VMEM((tp, hd), BF16),
                        pltpu.VMEM((tp, LANES), F32),
                        pltpu.VMEM((tp, LANES), F32),
                        pltpu.VMEM((2, tp, LANES), F32),
                        pltpu.VMEM((2, tp, LANES), F32),
                        pltpu.VMEM((2, tp, 2 * LANES), BF16),
                        pltpu.VMEM((nc, 8, LANES), F32),
                        pltpu.VMEM((2, nc, 8, LANES), F32),
                        pltpu.VMEM((2, tp, 2 * hd), F32),
                        pltpu.VMEM((2, tp, LANES), F32),
                        pltpu.VMEM((2, tp, LANES), F32),
                        pltpu.VMEM((2, tp, 2 * hd), BF16),
                        pltpu.VMEM((2, nc, 8, LANES), F32),
                        pltpu.VMEM((tp, hd), F32),
                        pltpu.VMEM((2, hd, 2 * hd), F32)],
        compiler_params=_cparams("parallel", "arbitrary"),
    )(proj, proj, proj, proj, gates, conv_w.astype(F32), conv_w.astype(F32),
      conv_b.astype(F32).reshape(1, -1), conv_b.astype(F32).reshape(1, -1), gb,
      norm_w.astype(F32).reshape(1, -1))


def _ssd_kernel(z_ref, x_ref, b_ref, c_ref, dt_ref, cwx_ref, cwb_ref, cwc_ref, cbx_ref, cbb_ref,
                cbc_ref, dtb_ref, alog_ref, dsk_ref, nw_ref, o_ref,
                src_ref, xs_ref, bc_ref, cc_ref, dts_ref, da_ref, sel_ref, rsel_ref,
                pk_ref, cum4_ref, xdt_ref, xw_ref, ecl_ref, y_ref, st_ref):
    tp = x_ref.shape[1]
    n_chunks = tp // CHUNK
    assert n_chunks % SCAN_UNROLL == 0
    grp = pl.program_id(1)
    gw = SSD_GW
    ns = SSD_STATE
    hdim = SSD_HEAD_DIM

    _fill_conv_src(src_ref, (x_ref, b_ref, c_ref), tp)
    st_ref[...] = jnp.zeros_like(st_ref)

    @pl.when(grp == 0)
    def _dt():
        row = lax.broadcasted_iota(jnp.int32, (tp, LANES), 0)
        lane_all = lax.broadcasted_iota(jnp.int32, (tp, LANES), 1)
        dt_all = jnp.where((row < PAD) | (lane_all >= 2 * SSD_HEADS), 0.0,
                           _softplus(dt_ref[0] + dtb_ref[...]))
        dts_ref[...] = dt_all
        da_ref[...] = dt_all * (-jnp.exp(alog_ref[...]))

        def loop_c(c, carry):
            rows = _chunk_rows(c)
            da = da_ref[rows, :]
            cumf = _cumsum_rows(da)
            pk_ref[0, rows, :] = _pack3(cumf)
            pk_ref[1, rows, :] = _pack3(cumf[CHUNK - 1:CHUNK] - cumf + da)
            pk_ref[2, rows, :] = _pack2(dts_ref[rows, :])
            return carry

        lax.fori_loop(0, n_chunks, loop_c, 0)

    col_head = lax.broadcasted_iota(jnp.int32, (gw, 2 * LANES), 0) // hdim
    row_head = lax.broadcasted_iota(jnp.int32, (8 * SSD_HPG, 2 * LANES), 0) // 8
    for d in range(2):
        lane0 = SSD_HEADS * d + SSD_HPG * grp
        sel_ref[d, 0] = _selector(lane0 + col_head, 3)
        sel_ref[d, 1] = _selector(lane0 + col_head, 2)
        rsel_ref[d] = _selector(lane0 + row_head, 3)

    conv_w = jnp.concatenate([cwx_ref[...], cwb_ref[...], cwc_ref[...]], axis=1)
    conv_b = jnp.concatenate([cbx_ref[...], cbb_ref[...], cbc_ref[...]], axis=1)

    def loop_a(c, carry):
        rows = _chunk_rows(c)
        a = _conv_silu(src_ref, conv_w, conv_b, c)
        xs_ref[rows, :] = a[:, :gw]
        bc_ref[rows, :] = a[:, gw:gw + ns].astype(BF16)
        cc_ref[rows, :] = a[:, gw + ns:].astype(BF16)
        return carry

    lax.fori_loop(0, n_chunks, loop_a, 0)
    xs_ref[0:PAD, :] = jnp.zeros((PAD, gw), F32)
    bc_ref[0:PAD, :] = jnp.zeros((PAD, ns), BF16)
    cc_ref[0:PAD, :] = jnp.zeros((PAD, ns), BF16)

    eb_rows = tp // EXPAND_STEPS

    def loop_e(i, carry):
        rows = pl.ds(pl.multiple_of(i * eb_rows, CHUNK), eb_rows)
        cum4 = [_nt(pk_ref[d, rows, :], sel_ref[d, 0]) for d in range(2)]
        dt4 = [_nt(pk_ref[2, rows, :], sel_ref[d, 1]) for d in range(2)]
        xs = xs_ref[rows, :]
        for d in range(2):
            cum4_ref[d, rows, :] = cum4[d]
            xdt_ref[d, rows, :] = xs * dt4[d]
        return carry

    lax.fori_loop(0, EXPAND_STEPS, loop_e, 0)

    blk4 = lax.broadcasted_iota(jnp.int32, (CHUNK, gw), 1) // hdim
    units = [(u, d) for u in range(SCAN_UNROLL) for d in range(2)]

    def loop_p(i, carry):
        cs = [i * SCAN_UNROLL + u for u in range(SCAN_UNROLL)]
        rows = [_chunk_rows(c) for c in cs]
        cb2 = [_nt(cc_ref[r, :], jnp.concatenate([bc_ref[r, :]] * 2, axis=0)) for r in rows]
        cb4 = [jnp.concatenate([x] * (SSD_HPG // 2), axis=1) for x in cb2]
        row32 = {(u, d): _nt(rsel_ref[d], pk_ref[d, rows[u], :]) for u, d in units}
        m4, xdiag = {}, {}
        for u, d in units:
            reverse = d == 1
            cum4 = cum4_ref[d, rows[u], :]
            r32 = row32[u, d]
            row_cum = jnp.concatenate([r32[8 * r:8 * r + 8] for r in range(SSD_HPG)], axis=1)
            seg4 = cum4 - jnp.concatenate([row_cum] * (CHUNK // 8), axis=0)
            decay4 = jnp.where(_scan_tri((CHUNK, gw), reverse), jnp.exp(seg4), 0.0)
            m4[u, d] = (decay4 * cb4[u]).astype(BF16)
            xdt = xdt_ref[d, rows[u], :]
            xdt_b = xdt.astype(BF16)
            xdiag[u, d] = jnp.concatenate([jnp.where(blk4 == r, xdt_b, jnp.zeros_like(xdt_b))
                                           for r in range(SSD_HPG)], axis=0)
            cum_last = cum4[0:1] if reverse else cum4[CHUNK - 1:CHUNK]
            xw_ref[d, rows[u], :] = (jnp.exp(cum_last - cum4) * xdt).astype(BF16)
            ecl_ref[d, cs[u]] = jnp.broadcast_to(jnp.exp(cum_last), (8, gw))
        intra = {k: jnp.dot(m4[k], xdiag[k], preferred_element_type=F32) for k in units}
        for u in range(SCAN_UNROLL):
            y_ref[rows[u], :] = dsk_ref[...] * xs_ref[rows[u], :] + intra[u, 0] + intra[u, 1]
        return carry

    lax.fori_loop(0, n_chunks // SCAN_UNROLL, loop_p, 0)

    def loop_s(i, carry):
        cs, rows = _scan_steps(i, n_chunks)
        upd = {k: _tn(bc_ref[rows[k], :], xw_ref[k[1], rows[k], :]) for k in units}
        st = {(0, d): st_ref[d] for d in range(2)}
        inter = {}
        for u in range(SCAN_UNROLL):
            for d in range(2):
                inter[u, d] = jnp.dot(cc_ref[rows[u, d], :], st[u, d].astype(BF16),
                                      preferred_element_type=F32)
                st[u + 1, d] = ecl_ref[d, cs[u, d]][0:1] * st[u, d] + upd[u, d]
        for u, d in units:
            y_ref[rows[u, d], :] += jnp.exp(cum4_ref[d, rows[u, d], :]) * inter[u, d]
        for d in range(2):
            st_ref[d] = st[SCAN_UNROLL, d]
        return carry

    lax.fori_loop(0, n_chunks // SCAN_UNROLL, loop_s, 0)

    rowg = lax.broadcasted_iota(jnp.int32, (tp, gw), 0)
    y = y_ref[...] * _silu(z_ref[0].astype(F32))
    y = y * lax.rsqrt(jnp.mean(y * y, axis=1, keepdims=True) + RMS_EPS) * nw_ref[...]
    o_ref[0] = jnp.where(rowg < PAD, 0.0, y).astype(BF16)


def _ssd(proj, dt_raw, conv_w, conv_b, dt_bias, a_log, d_skip, norm_w):
    bsz, tp, _ = proj.shape
    gw, ns = SSD_GW, SSD_STATE
    nc = tp // CHUNK
    xb, bb, cb = D_INNER // gw, 2 * D_INNER // ns, 2 * D_INNER // ns + SSD_GROUPS

    def col_block(width, j0):
        return pl.BlockSpec((1, tp, width), lambda b, g: (b, 0, j0 + g))

    def par_block(nrow, width, j0):
        return pl.BlockSpec((nrow, width), lambda b, g: (0, j0 + g))

    def lane_row(v):
        v = v.astype(F32).reshape(-1)
        return jnp.pad(v, (0, LANES - v.shape[0])).reshape(1, LANES)

    cw = conv_w.astype(F32)
    cbias = conv_b.astype(F32).reshape(1, -1)
    xb_c, bb_c, cb_c = 0, D_INNER // ns, D_INNER // ns + SSD_GROUPS
    dsk = jnp.repeat(d_skip.astype(F32), SSD_HEAD_DIM).reshape(1, D_INNER)
    full_row = pl.BlockSpec((1, LANES), lambda b, g: (0, 0))
    return pl.pallas_call(
        _ssd_kernel,
        grid=(bsz, SSD_GROUPS),
        in_specs=[col_block(gw, 0), col_block(gw, xb), col_block(ns, bb), col_block(ns, cb),
                  pl.BlockSpec((1, tp, LANES), lambda b, g: (b, 0, 0)),
                  par_block(CONV_W, gw, xb_c), par_block(CONV_W, ns, bb_c), par_block(CONV_W, ns, cb_c),
                  par_block(1, gw, xb_c), par_block(1, ns, bb_c), par_block(1, ns, cb_c),
                  full_row, full_row, par_block(1, gw, 0), par_block(1, gw, 0)],
        out_specs=pl.BlockSpec((1, tp, gw), lambda b, g: (b, 0, g)),
        out_shape=jax.ShapeDtypeStruct((bsz, tp, D_INNER), BF16),
        scratch_shapes=[pltpu.VMEM((tp + 16, gw + 2 * ns), F32),
                        pltpu.VMEM((tp, gw), F32),
                        pltpu.VMEM((tp, ns), BF16),
                        pltpu.VMEM((tp, ns), BF16),
                        pltpu.VMEM((tp, LANES), F32),
                        pltpu.VMEM((tp, LANES), F32),
                        pltpu.VMEM((2, 2, gw, 2 * LANES), BF16),
                        pltpu.VMEM((2, 8 * SSD_HPG, 2 * LANES), BF16),
                        pltpu.VMEM((3, tp, 2 * LANES), BF16),
                        pltpu.VMEM((2, tp, gw), F32),
                        pltpu.VMEM((2, tp, gw), F32),
                        pltpu.VMEM((2, tp, gw), BF16),
                        pltpu.VMEM((2, nc, 8, gw), F32),
                        pltpu.VMEM((tp, gw), F32),
                        pltpu.VMEM((2, ns, gw), F32)],
        compiler_params=_cparams("parallel", "arbitrary"),
    )(proj, proj, proj, proj, dt_raw, cw, cw, cw, cbias, cbias, cbias,
      lane_row(dt_bias), lane_row(a_log), dsk, norm_w.astype(F32).reshape(1, -1))


def _final_kernel(x_ref, nw_ref, o_ref):
    seq = o_ref.shape[1]
    step = 256
    for r0 in range(0, seq, step):
        x = x_ref[0, PAD + N_META + r0:PAD + N_META + r0 + step, :]
        o_ref[0, r0:r0 + step, :] = _rms_rows(x, nw_ref[...])


def _final_norm(h3, norm_w, seq):
    bsz, tp, _ = h3.shape
    return pl.pallas_call(
        _final_kernel,
        grid=(bsz,),
        in_specs=[pl.BlockSpec((1, tp, D_MODEL), lambda b: (b, 0, 0)), _resident((1, D_MODEL))],
        out_specs=pl.BlockSpec((1, seq, D_MODEL), lambda b: (b, 0, 0)),
        out_shape=jax.ShapeDtypeStruct((bsz, seq, D_MODEL), F32),
        compiler_params=_cparams("parallel"),
    )(h3, norm_w.reshape(1, D_MODEL))


def _even_mixer(h2, bsz, tp, norm_w, w_in, rpb, conv_w, conv_b, gate_bias, ml_norm_w, w_out):
    proj, gates = _norm_proj(h2, norm_w, w_in, EV_MAIN)
    proj = proj.reshape(bsz, tp, EV_MAIN)
    y_a = _natten(proj, _na_bias_table(rpb))
    y_b = _mlstm(proj, gates.reshape(bsz, tp, LANES), conv_w, conv_b, gate_bias, ml_norm_w)
    return _out_proj([y_a.reshape(bsz * tp, D_A), y_b.reshape(bsz * tp, D_B)],
                     [w_out[:D_A], w_out[D_A:]], h2)


def _odd_mixer(h2, bsz, tp, norm_w, w_in, conv_w, conv_b, dt_bias, a_log, d_skip, ssd_norm_w, w_out):
    proj, dt_raw = _norm_proj(h2, norm_w, w_in, OD_MAIN)
    y = _ssd(proj.reshape(bsz, tp, OD_MAIN), dt_raw.reshape(bsz, tp, LANES), conv_w, conv_b,
             dt_bias, a_log, d_skip, ssd_norm_w)
    return _out_proj([y.reshape(bsz * tp, D_INNER)], [w_out], h2)


def kernel(x, meta_tokens, ffn1_norm, ffn1_gate, ffn1_up, ffn1_down, mix_norm, ffn2_norm, ffn2_gate, ffn2_up, ffn2_down, ev_w_in, na_rpb, ml_conv_w, ml_conv_b, ml_gate_bias, ml_norm_w, ev_w_out, od_w_in, ssd_conv_w, ssd_conv_b, ssd_dt_bias, ssd_a_log, ssd_d, ssd_norm_w, od_w_out, final_norm):
    bsz, seq, _ = x.shape
    tp = PAD + N_META + seq
    assert x.shape[2] == D_MODEL and seq % GRID_W == 0 and tp % ROW_TILE == 0
    assert seq // GRID_W >= NA_WIN_H
    head = jnp.concatenate([jnp.zeros((PAD, D_MODEL), F32), meta_tokens.astype(F32)], axis=0)
    h = jnp.concatenate([jnp.broadcast_to(head[None], (bsz, PAD + N_META, D_MODEL)), x], axis=1)
    h2 = h.reshape(bsz * tp, D_MODEL)
    for layer in range(ffn1_norm.shape[0]):
        h2 = _ffn(h2, ffn1_norm[layer], ffn1_gate[layer], ffn1_up[layer], ffn1_down[layer])
        i = layer // 2
        if layer % 2 == 0:
            h2 = _even_mixer(h2, bsz, tp, mix_norm[layer], ev_w_in[i], na_rpb[i], ml_conv_w[i],
                             ml_conv_b[i], ml_gate_bias[i], ml_norm_w[i], ev_w_out[i])
        else:
            h2 = _odd_mixer(h2, bsz, tp, mix_norm[layer], od_w_in[i], ssd_conv_w[i], ssd_conv_b[i],
                            ssd_dt_bias[i], ssd_a_log[i], ssd_d[i], ssd_norm_w[i], od_w_out[i])
        h2 = _ffn(h2, ffn2_norm[layer], ffn2_gate[layer], ffn2_up[layer], ffn2_down[layer])
    return _final_norm(h2.reshape(bsz, tp, D_MODEL), final_norm, seq)
```

```python
import functools

import jax
import jax.numpy as jnp
import numpy as np
from jax import lax
from jax.experimental import pallas as pl
from jax.experimental.pallas import tpu as pltpu

F32 = jnp.float32
BF16 = jnp.bfloat16

D_MODEL = 1024
N_META = 16
GRID_W = 64
CHUNK = 64
PAD = CHUNK - N_META
CONV_W = 5
CONV_HALF = (CONV_W - 1) // 2
D_FF = 2816
RMS_EPS = 1e-6

NA_HEADS = 8
NA_HEAD_DIM = 64
NA_WIN_H = 8
NA_WIN_W = 16
ML_HEADS = 4
ML_HEAD_DIM = 128
D_A = NA_HEADS * NA_HEAD_DIM
D_B = ML_HEADS * ML_HEAD_DIM
EV_MAIN = 3 * D_A + 4 * D_B

SSD_HEAD_DIM = 64
SSD_HEADS = 32
SSD_GROUPS = 8
SSD_HPG = SSD_HEADS // SSD_GROUPS
SSD_STATE = 128
D_INNER = SSD_HEADS * SSD_HEAD_DIM
SSD_GW = SSD_HPG * SSD_HEAD_DIM
OD_MAIN = 2 * D_INNER + 2 * SSD_GROUPS * SSD_STATE

LANES = 128
HALF = LANES // 2
FF_CHUNK = 256
ROW_TILE = 528
FFN_ROW_TILE = 1056
NA_ROWS_PER_STEP = 2
SCAN_UNROLL = 33
EXPAND_STEPS = 3
VMEM_LIMIT = 56 * 1024 * 1024


def _cparams(*sem):
    return pltpu.CompilerParams(dimension_semantics=sem, vmem_limit_bytes=VMEM_LIMIT)


def _resident(shape):
    nd = len(shape)
    return pl.BlockSpec(shape, lambda *_: (0,) * nd, pipeline_mode=pl.Buffered(1))


def _rms_rows(x, w_row):
    ms = jnp.mean(x * x, axis=-1, keepdims=True)
    return x * lax.rsqrt(ms + RMS_EPS) * w_row


def _sigmoid(x):
    return 1.0 / (1.0 + jnp.exp(-x))


def _silu(x):
    return x * _sigmoid(x)


def _softplus(x):
    return jnp.maximum(x, 0.0) + jnp.log(1.0 + jnp.exp(-jnp.abs(x)))


def _log_sigmoid(x):
    return -_softplus(-x)


def _nt(a, b):
    return lax.dot_general(a, b, (((1,), (1,)), ((), ())), preferred_element_type=F32)


def _tn(a, b):
    return lax.dot_general(a, b, (((0,), (0,)), ((), ())), preferred_element_type=F32)


def _ffn_kernel(x_ref, nw_ref, wg_ref, wu_ref, wd_ref, o_ref, xn_ref, acc_ref):
    xn_ref[...] = _rms_rows(x_ref[...], nw_ref[...]).astype(BF16)
    acc_ref[...] = jnp.zeros_like(acc_ref)
    for f in range(D_FF // FF_CHUNK):
        cols = slice(f * FF_CHUNK, (f + 1) * FF_CHUNK)
        g = jnp.dot(xn_ref[...], wg_ref[:, cols], preferred_element_type=F32)
        u = jnp.dot(xn_ref[...], wu_ref[:, cols], preferred_element_type=F32)
        acc_ref[...] += jnp.dot((_silu(g) * u).astype(BF16), wd_ref[cols, :], preferred_element_type=F32)
    o_ref[...] = x_ref[...] + 0.5 * acc_ref[...]


def _ffn_final_kernel(x_ref, nw_ref, wg_ref, wu_ref, wd_ref, fw_ref, o_ref, xn_ref, acc_ref):
    xn_ref[...] = _rms_rows(x_ref[...], nw_ref[...]).astype(BF16)
    acc_ref[...] = jnp.zeros_like(acc_ref)
    for f in range(D_FF // FF_CHUNK):
        cols = slice(f * FF_CHUNK, (f + 1) * FF_CHUNK)
        g = jnp.dot(xn_ref[...], wg_ref[:, cols], preferred_element_type=F32)
        u = jnp.dot(xn_ref[...], wu_ref[:, cols], preferred_element_type=F32)
        acc_ref[...] += jnp.dot((_silu(g) * u).astype(BF16), wd_ref[cols, :], preferred_element_type=F32)
    o_ref[0] = _rms_rows(x_ref[...] + 0.5 * acc_ref[...], fw_ref[...])


def _ffn_final(h3, norm_w, w_gate, w_up, w_down, final_w, seq):
    bsz, tp, _ = h3.shape
    tile = seq // 2
    return pl.pallas_call(
        _ffn_final_kernel,
        grid=(bsz, seq // tile),
        in_specs=[pl.BlockSpec((pl.Element(tile), pl.Element(D_MODEL)),
                               lambda b, j: (pl.multiple_of(b * tp + PAD + N_META + j * tile, 8), 0)),
                  _resident((1, D_MODEL)), _resident(w_gate.shape), _resident(w_up.shape),
                  _resident(w_down.shape), _resident((1, D_MODEL))],
        out_specs=pl.BlockSpec((1, tile, D_MODEL), lambda b, j: (b, j, 0)),
        out_shape=jax.ShapeDtypeStruct((bsz, seq, D_MODEL), F32),
        scratch_shapes=[pltpu.VMEM((tile, D_MODEL), BF16), pltpu.VMEM((tile, D_MODEL), F32)],
        compiler_params=_cparams("parallel", "parallel"),
    )(h3.reshape(bsz * tp, D_MODEL), norm_w.reshape(1, D_MODEL), w_gate.astype(BF16), w_up.astype(BF16),
      w_down.astype(BF16), final_w.reshape(1, D_MODEL))


def _ffn(h2, norm_w, w_gate, w_up, w_down):
    rows = h2.shape[0]
    row_spec = pl.BlockSpec((FFN_ROW_TILE, D_MODEL), lambda i: (i, 0))
    return pl.pallas_call(
        _ffn_kernel,
        grid=(rows // FFN_ROW_TILE,),
        in_specs=[row_spec, _resident((1, D_MODEL)), _resident(w_gate.shape), _resident(w_up.shape),
                  _resident(w_down.shape)],
        out_specs=row_spec,
        out_shape=jax.ShapeDtypeStruct(h2.shape, F32),
        scratch_shapes=[pltpu.VMEM((FFN_ROW_TILE, D_MODEL), BF16), pltpu.VMEM((FFN_ROW_TILE, D_MODEL), F32)],
        compiler_params=_cparams("parallel"),
    )(h2, norm_w.reshape(1, D_MODEL), w_gate.astype(BF16), w_up.astype(BF16), w_down.astype(BF16))


def _norm_proj_kernel(x_ref, nw_ref, w_ref, ws_ref, o_ref, os_ref, xn_ref, *, col_chunk):
    xn_ref[...] = _rms_rows(x_ref[...], nw_ref[...]).astype(BF16)
    for c in range(w_ref.shape[1] // col_chunk):
        cols = slice(c * col_chunk, (c + 1) * col_chunk)
        o_ref[:, cols] = jnp.dot(xn_ref[...], w_ref[:, cols], preferred_element_type=F32).astype(BF16)
    os_ref[...] = jnp.dot(xn_ref[...], ws_ref[...], preferred_element_type=F32)


def _norm_proj(h2, norm_w, w_in, n_main):
    rows = h2.shape[0]
    w_main = w_in[:, :n_main].astype(BF16)
    n_small = w_in.shape[1] - n_main
    w_small = jnp.pad(w_in[:, n_main:], ((0, 0), (0, LANES - n_small))).astype(BF16)
    row_spec = pl.BlockSpec((ROW_TILE, D_MODEL), lambda i: (i, 0))
    return pl.pallas_call(
        functools.partial(_norm_proj_kernel, col_chunk=2 * FF_CHUNK),
        grid=(rows // ROW_TILE,),
        in_specs=[row_spec, _resident((1, D_MODEL)), _resident(w_main.shape), _resident(w_small.shape)],
        out_specs=[pl.BlockSpec((ROW_TILE, n_main), lambda i: (i, 0)),
                   pl.BlockSpec((ROW_TILE, LANES), lambda i: (i, 0))],
        out_shape=[jax.ShapeDtypeStruct((rows, n_main), BF16),
                   jax.ShapeDtypeStruct((rows, LANES), F32)],
        scratch_shapes=[pltpu.VMEM((ROW_TILE, D_MODEL), BF16)],
        compiler_params=_cparams("parallel"),
    )(h2, norm_w.reshape(1, D_MODEL), w_main, w_small)


def _out_proj_kernel(*refs):
    n = (len(refs) - 2) // 2
    h_ref, o_ref = refs[2 * n], refs[2 * n + 1]
    acc = h_ref[...]
    for y_ref, w_ref in zip(refs[:n], refs[n:2 * n]):
        acc = acc + jnp.dot(y_ref[...], w_ref[...], preferred_element_type=F32)
    o_ref[...] = acc


def _out_proj(ys, ws, h2):
    rows = h2.shape[0]
    row_spec = pl.BlockSpec((ROW_TILE, D_MODEL), lambda i: (i, 0))
    ws = [w.astype(BF16) for w in ws]
    return pl.pallas_call(
        _out_proj_kernel,
        grid=(rows // ROW_TILE,),
        in_specs=([pl.BlockSpec((ROW_TILE, y.shape[1]), lambda i: (i, 0)) for y in ys]
                  + [_resident(w.shape) for w in ws] + [row_spec]),
        out_specs=row_spec,
        out_shape=jax.ShapeDtypeStruct(h2.shape, F32),
        compiler_params=_cparams("parallel"),
    )(*ys, *ws, h2)


def _na_bias_table(rpb):
    qc = np.arange(GRID_W)[:, None]
    kc = np.arange(GRID_W)[None, :]
    dc = np.clip(kc - qc, 1 - NA_WIN_W, NA_WIN_W - 1) + (NA_WIN_W - 1)
    win_c0 = np.clip(qc - NA_WIN_W // 2, 0, GRID_W - NA_WIN_W)
    col_ok = (kc >= win_c0) & (kc < win_c0 + NA_WIN_W)
    onehot = (dc[:, :, None] == np.arange(2 * NA_WIN_W - 1)).astype(np.float32)
    toep = jnp.einsum('hdc,qkc->hdqk', rpb.astype(F32), onehot, precision=lax.Precision.HIGHEST)
    toep = jnp.where(col_ok[None, None], toep, -jnp.inf)
    tbl = jnp.stack([toep[:, d0:d0 + NA_WIN_H] for d0 in range(NA_WIN_H)], axis=1)
    tbl = tbl.transpose(0, 1, 3, 2, 4).reshape(NA_HEADS, NA_WIN_H, GRID_W, NA_WIN_H * GRID_W)
    meta = jnp.zeros((NA_HEADS, NA_WIN_H, GRID_W, N_META), F32)
    return jnp.concatenate([tbl, meta], axis=-1)


def _natten_kernel(q_ref, k_ref, v_ref, bias_ref, o_ref, *, n_rows):
    scale = NA_HEAD_DIM ** -0.5
    g0 = PAD + N_META
    n_keys = NA_WIN_H * GRID_W

    pairs = [slice(j * LANES, (j + 1) * LANES) for j in range(NA_HEADS // 2)]

    def attend(jobs):
        low = lax.broadcasted_iota(jnp.int32, (jobs[0][0].shape[0], LANES), 1) < NA_HEAD_DIM
        s = []
        for q, k, _, _ in jobs:
            for c in pairs:
                qp = q[:, c]
                s.append(_nt(jnp.where(low, qp, jnp.zeros_like(qp)), k[:, c]))
                s.append(_nt(jnp.where(low, jnp.zeros_like(qp), qp), k[:, c]))
        p, l = [], []
        for i, (_, _, _, bias) in enumerate(jobs):
            for h in range(NA_HEADS):
                sh = s[i * NA_HEADS + h] * scale
                if bias is not None:
                    sh = sh + bias(h)
                ph = jnp.exp(sh - jnp.max(sh, axis=1, keepdims=True))
                l.append(jnp.sum(ph, axis=1, keepdims=True))
                p.append(ph.astype(BF16))
        o = [jnp.dot(p[i * NA_HEADS + h], v[:, pairs[h // 2]], preferred_element_type=F32)
             for i, (_, _, v, _) in enumerate(jobs) for h in range(NA_HEADS)]
        o = [x / y for x, y in zip(o, l)]
        return [jnp.concatenate([jnp.where(low, o[i * NA_HEADS + 2 * j], o[i * NA_HEADS + 2 * j + 1])
                                 for j in range(NA_HEADS // 2)], axis=1).astype(BF16)
                for i in range(len(jobs))]

    k_meta = k_ref[0, PAD:g0, :]
    v_meta = v_ref[0, PAD:g0, :]
    o_ref[0, :PAD, :] = jnp.zeros((PAD, D_A), BF16)
    o_ref[0, PAD:g0, :] = attend([(q_ref[0, PAD:g0, :], k_meta, v_meta, None)])[0]

    def row_job(r):
        first = jnp.clip(r - NA_WIN_H // 2, 0, n_rows - NA_WIN_H)
        d0 = first - r + (NA_WIN_H - 1)
        q_rows = pl.ds(pl.multiple_of(g0 + r * GRID_W, GRID_W), GRID_W)
        k_rows = pl.ds(pl.multiple_of(g0 + first * GRID_W, GRID_W), n_keys)
        k = jnp.concatenate([k_ref[0, k_rows, :], k_meta], axis=0)
        v = jnp.concatenate([v_ref[0, k_rows, :], v_meta], axis=0)
        return q_rows, (q_ref[0, q_rows, :], k, v, lambda h: bias_ref[h, d0])

    def rows_body(i, carry):
        q_rows, jobs = zip(*[row_job(i * NA_ROWS_PER_STEP + j) for j in range(NA_ROWS_PER_STEP)])
        for rows, o in zip(q_rows, attend(list(jobs))):
            o_ref[0, rows, :] = o
        return carry

    lax.fori_loop(0, n_rows // NA_ROWS_PER_STEP, rows_body, 0)


def _natten(proj, bias_tbl):
    bsz, tp, _ = proj.shape
    n_rows = (tp - PAD - N_META) // GRID_W

    def col_block(j):
        return pl.BlockSpec((1, tp, D_A), lambda b: (b, 0, j))

    return pl.pallas_call(
        functools.partial(_natten_kernel, n_rows=n_rows),
        grid=(bsz,),
        in_specs=[col_block(0), col_block(1), col_block(2), _resident(bias_tbl.shape)],
        out_specs=pl.BlockSpec((1, tp, D_A), lambda b: (b, 0, 0)),
        out_shape=jax.ShapeDtypeStruct((bsz, tp, D_A), BF16),
        compiler_params=_cparams("parallel"),
    )(proj, proj, proj, bias_tbl)


def _fill_conv_src(src_ref, parts, tp):
    width = src_ref.shape[1]
    src_ref[0:8 + PAD, :] = jnp.zeros((8 + PAD, width), F32)
    src_ref[8 + tp:16 + tp, :] = jnp.zeros((8, width), F32)
    off = 0
    for p in parts:
        w = p.shape[2]
        src_ref[8 + PAD:8 + tp, off:off + w] = p[0, PAD:tp, :].astype(F32)
        off += w


def _conv_silu(src_ref, w, b, c):
    x = src_ref[pl.ds(pl.multiple_of(c * CHUNK, CHUNK), CHUNK + 16), :]
    acc = b + w[0:1, :] * x[8 - CONV_HALF:8 - CONV_HALF + CHUNK]
    for j in range(1, CONV_W):
        lo = 8 - CONV_HALF + j
        acc = acc + w[j:j + 1, :] * x[lo:lo + CHUNK]
    return _silu(acc)


def _chunk_rows(c):
    return pl.ds(pl.multiple_of(c * CHUNK, CHUNK), CHUNK)


def _scan_steps(i, n_chunks):
    cs = {}
    for u in range(SCAN_UNROLL):
        cs[u, 0] = i * SCAN_UNROLL + u
        cs[u, 1] = n_chunks - 1 - cs[u, 0]
    return cs, {k: _chunk_rows(c) for k, c in cs.items()}


def _scan_tri(shape, reverse):
    t = lax.broadcasted_iota(jnp.int32, shape, 0)
    s = lax.broadcasted_iota(jnp.int32, shape, 1) % CHUNK
    return (s >= t) if reverse else (s <= t)


def _cumsum_rows(x):
    row = lax.broadcasted_iota(jnp.int32, x.shape, 0)
    sh = 1
    while sh < CHUNK:
        x = x + jnp.where(row >= sh, pltpu.roll(x, sh, axis=0), 0.0)
        sh *= 2
    return x


def _pack3(x):
    hi = x.astype(BF16).astype(F32)
    r1 = x - hi
    mid = r1.astype(BF16).astype(F32)
    return jnp.concatenate([hi + pltpu.roll(mid, HALF, axis=1), r1 - mid], axis=1).astype(BF16)


def _pack2(x):
    hi = x.astype(BF16)
    return jnp.concatenate([hi, (x - hi.astype(F32)).astype(BF16)], axis=1)


def _selector(lane_of_row, terms):
    k = lax.broadcasted_iota(jnp.int32, lane_of_row.shape, 1)
    hit = (k == lane_of_row) | (k == lane_of_row + LANES)
    if terms == 3:
        hit = hit | (k == lane_of_row + HALF)
    return jnp.where(hit, 1.0, 0.0).astype(BF16)


def _mlstm_kernel(q_ref, k_ref, v_ref, og_ref, gates_ref, cwq_ref, cwk_ref, cbq_ref, cbk_ref,
                  gb_ref, nw_ref, o_ref, src_ref, qc_ref, kc_ref, x_ref, lf_ref, g_ref, w_ref, pkw_ref,
                  stat_ref, m_ref, a_ref, wi_ref, em_ref, wv_ref, wo_ref, y_ref, st_ref):
    tp = q_ref.shape[1]
    n_chunks = tp // CHUNK
    head = pl.program_id(1)
    hd = ML_HEAD_DIM

    _fill_conv_src(src_ref, (q_ref, k_ref), tp)
    row = lax.broadcasted_iota(jnp.int32, (tp, LANES), 0)
    st_ref[...] = jnp.zeros_like(st_ref)
    y_ref[...] = jnp.zeros_like(y_ref)

    @pl.when(head == 0)
    def _gates():
        x_all = jnp.where(row < PAD, 0.0, gates_ref[0] + gb_ref[...])
        x_ref[...] = x_all
        lf_ref[...] = _log_sigmoid(x_all)

        glane = lax.broadcasted_iota(jnp.int32, (CHUNK, LANES), 1)

        def loop_g(c, carry):
            rows = _chunk_rows(c)
            lf = lf_ref[rows, :]
            gf = _cumsum_rows(lf)
            xi = pltpu.roll(x_ref[rows, :], ML_HEADS, axis=1)
            tot = gf[CHUNK - 1:CHUNK]
            stats = []
            for d, g in enumerate((gf, tot - gf + lf)):
                w = xi - g
                g_ref[d, rows, :] = g
                w_ref[d, rows, :] = w
                pkw_ref[d, rows, :] = _pack3(jnp.where(glane < HALF, w, 0.0))
                stats += [tot, jnp.max(tot + w, axis=0, keepdims=True)]
            stat_ref[c] = jnp.concatenate(stats + [jnp.zeros((4, LANES), F32)], axis=0)
            return carry

        lax.fori_loop(0, n_chunks, loop_g, 0)

        def loop_m(i, m):
            mf, mb = m
            sf = stat_ref[i]
            nf = jnp.maximum(sf[0:1] + mf, sf[1:2])
            m_ref[0, i] = jnp.concatenate([mf, nf, jnp.zeros((6, LANES), F32)], axis=0)
            cb = n_chunks - 1 - i
            sb = stat_ref[cb]
            nb = jnp.maximum(sb[2:3] + mb, sb[3:4])
            m_ref[1, cb] = jnp.concatenate([mb, nb, jnp.zeros((6, LANES), F32)], axis=0)
            return nf, nb

        zero_row = jnp.zeros((1, LANES), F32)
        lax.fori_loop(0, n_chunks, loop_m, (zero_row, zero_row))

    conv_w = jnp.concatenate([cwq_ref[...], cwk_ref[...]], axis=1)
    conv_b = jnp.concatenate([cbq_ref[...], cbk_ref[...]], axis=1)

    def loop_a(c, carry):
        rows = _chunk_rows(c)
        qk = _conv_silu(src_ref, conv_w, conv_b, c)
        qc_ref[rows, :] = qk[:, :hd].astype(BF16)
        kc_ref[rows, :] = (qk[:, hd:] * (hd ** -0.5)).astype(BF16)
        return carry

    lax.fori_loop(0, n_chunks, loop_a, 0)
    qc_ref[0:PAD, :] = jnp.zeros((PAD, hd), BF16)
    kc_ref[0:PAD, :] = jnp.zeros((PAD, hd), BF16)

    lane = lax.broadcasted_iota(jnp.int32, (CHUNK, LANES), 1)
    lane1 = lax.broadcasted_iota(jnp.int32, (1, LANES), 1)
    vrow = lax.broadcasted_iota(jnp.int32, (CHUNK, hd), 0)

    def pick(x, lane_ids, f_lane):
        return jnp.sum(jnp.where(lane_ids == f_lane, x, 0.0), axis=1, keepdims=True)

    units = [(u, d) for u in range(SCAN_UNROLL) for d in range(2)]
    f_lanes = [2 * ML_HEADS * d + ML_HEADS + head for d in range(2)]
    sels = [_selector(jnp.full((8, 2 * LANES), f, jnp.int32), 3) for f in f_lanes]

    def loop_p(i, carry):
        cs = [i * SCAN_UNROLL + u for u in range(SCAN_UNROLL)]
        rows = [_chunk_rows(c) for c in cs]
        qk = [_nt(qc_ref[r, :], kc_ref[r, :]) for r in rows]
        w_row = {(u, d): _nt(sels[d], pkw_ref[d, rows[u], :]) for u, d in units}
        v1, g_col, w_col = [], {}, {}
        for u in range(SCAN_UNROLL):
            v = jnp.where(vrow + cs[u] * CHUNK < PAD, 0.0, v_ref[0, rows[u], :].astype(F32))
            v1.append(jnp.concatenate([v, jnp.ones((CHUNK, hd), F32)], axis=1))
            for d in range(2):
                g_col[u, d] = pick(g_ref[d, rows[u], :], lane, f_lanes[d])
                w_col[u, d] = pick(w_ref[d, rows[u], :], lane, f_lanes[d])
        sw = {}
        for u, d in units:
            reverse = d == 1
            gc = g_col[u, d]
            dmat = gc + jnp.concatenate([w_row[u, d]] * (CHUNK // 8), axis=0)
            dmat = jnp.where(_scan_tri((CHUNK, CHUNK), reverse), dmat, -jnp.inf)
            mm = m_ref[d, cs[u]]
            m_st = pick(mm[0:1], lane1, f_lanes[d])
            m_new = pick(mm[1:2], lane1, f_lanes[d])
            g_last = gc[0:1] if reverse else gc[CHUNK - 1:CHUNK]
            m_inter = gc + m_st
            m_t = jnp.maximum(m_inter, jnp.max(dmat, axis=1, keepdims=True))
            sw[u, d] = (jnp.exp(dmat - m_t) * qk[u]).astype(BF16)
            wi_ref[d, rows[u], :] = jnp.broadcast_to(jnp.exp(m_inter - m_t), (CHUNK, LANES))
            em_ref[d, rows[u], :] = jnp.broadcast_to(jnp.exp(-m_t), (CHUNK, LANES))
            w_end = jnp.exp(g_last + w_col[u, d] - m_new)
            wv_ref[d, rows[u], :] = (w_end * v1[u]).astype(BF16)
            wo_ref[d, cs[u]] = jnp.broadcast_to(jnp.exp(g_last + m_st - m_new), (8, LANES))
        v1b = [x.astype(BF16) for x in v1]
        intra = {(u, d): jnp.dot(sw[u, d], v1b[u], preferred_element_type=F32) for u, d in units}
        for u, d in units:
            a_ref[d, rows[u], :] = intra[u, d]
        return carry

    lax.fori_loop(0, n_chunks // SCAN_UNROLL, loop_p, 0)

    def loop_s(i, carry):
        cs, rows = _scan_steps(i, n_chunks)
        upd = {k: _tn(kc_ref[rows[k], :], wv_ref[k[1], rows[k], :]) for k in units}
        st = {(0, d): st_ref[d] for d in range(2)}
        inter = {}
        for u in range(SCAN_UNROLL):
            for d in range(2):
                inter[u, d] = jnp.dot(qc_ref[rows[u, d], :], st[u, d].astype(BF16),
                                      preferred_element_type=F32)
                st[u + 1, d] = wo_ref[d, cs[u, d]][0:1, 0:1] * st[u, d] + upd[u, d]
        for u, d in units:
            r = rows[u, d]
            wi = wi_ref[d, r, :]
            tot = a_ref[d, r, :] + jnp.concatenate([wi, wi], axis=1) * inter[u, d]
            y_ref[r, :] += tot[:, :hd] / jnp.maximum(jnp.abs(tot[:, hd:]), em_ref[d, r, :])
        for d in range(2):
            st_ref[d] = st[SCAN_UNROLL, d]
        return carry

    lax.fori_loop(0, n_chunks // SCAN_UNROLL, loop_s, 0)

    y = y_ref[...]
    y = y * lax.rsqrt(jnp.mean(y * y, axis=1, keepdims=True) + RMS_EPS) * nw_ref[...]
    y = y * _sigmoid(og_ref[0].astype(F32))
    o_ref[0] = jnp.where(row < PAD, 0.0, y).astype(BF16)


def _mlstm(proj, gates, conv_w, conv_b, gate_bias, norm_w):
    bsz, tp, _ = proj.shape
    hd = ML_HEAD_DIM
    nc = tp // CHUNK
    base = 3 * D_A // hd

    def col_block(j0):
        return pl.BlockSpec((1, tp, hd), lambda b, h: (b, 0, j0 + h))

    def par_block(nrow, j0):
        return pl.BlockSpec((nrow, hd), lambda b, h: (0, j0 + h))

    gb = jnp.pad(gate_bias.astype(F32), (0, LANES - gate_bias.shape[0])).reshape(1, LANES)
    return pl.pallas_call(
        _mlstm_kernel,
        grid=(bsz, ML_HEADS),
        in_specs=[col_block(base), col_block(base + ML_HEADS), col_block(base + 2 * ML_HEADS),
                  col_block(base + 3 * ML_HEADS),
                  pl.BlockSpec((1, tp, LANES), lambda b, h: (b, 0, 0)),
                  par_block(CONV_W, 0), par_block(CONV_W, ML_HEADS),
                  par_block(1, 0), par_block(1, ML_HEADS),
                  pl.BlockSpec((1, LANES), lambda b, h: (0, 0)),
                  par_block(1, 0)],
        out_specs=pl.BlockSpec((1, tp, hd), lambda b, h: (b, 0, h)),
        out_shape=jax.ShapeDtypeStruct((bsz, tp, D_B), BF16),
        scratch_shapes=[pltpu.VMEM((tp + 16, 2 * hd), F32),
                        pltpu.VMEM((tp, hd), BF16),
                        pltpu.VMEM((tp, hd), BF16),
                        pltpu.VMEM((tp, LANES), F32),
                        pltpu.VMEM((tp, LANES), F32),
                        pltpu.VMEM((2, tp, LANES), F32),
                        pltpu.VMEM((2, tp, LANES), F32),
                        pltpu.VMEM((2, tp, 2 * LANES), BF16),
                        pltpu.VMEM((nc, 8, LANES), F32),
                        pltpu.VMEM((2, nc, 8, LANES), F32),
                        pltpu.VMEM((2, tp, 2 * hd), F32),
                        pltpu.VMEM((2, tp, LANES), F32),
                        pltpu.VMEM((2, tp, LANES), F32),
                        pltpu.VMEM((2, tp, 2 * hd), BF16),
                        pltpu.VMEM((2, nc, 8, LANES), F32),
                        pltpu.VMEM((tp, hd), F32),
                        pltpu.VMEM((2, hd, 2 * hd), F32)],
        compiler_params=_cparams("parallel", "arbitrary"),
    )(proj, proj, proj, proj, gates, conv_w.astype(F32), conv_w.astype(F32),
      conv_b.astype(F32).reshape(1, -1), conv_b.astype(F32).reshape(1, -1), gb,
      norm_w.astype(F32).reshape(1, -1))


def _ssd_kernel(z_ref, x_ref, b_ref, c_ref, dt_ref, cwx_ref, cwb_ref, cwc_ref, cbx_ref, cbb_ref,
                cbc_ref, dtb_ref, alog_ref, dsk_ref, nw_ref, o_ref,
                src_ref, xs_ref, bc_ref, cc_ref, dts_ref, da_ref, sel_ref, rsel_ref,
                pk_ref, cum4_ref, xdt_ref, xw_ref, ecl_ref, y_ref, st_ref):
    tp = x_ref.shape[1]
    n_chunks = tp // CHUNK
    assert n_chunks % SCAN_UNROLL == 0
    grp = pl.program_id(1)
    gw = SSD_GW
    ns = SSD_STATE
    hdim = SSD_HEAD_DIM

    _fill_conv_src(src_ref, (x_ref, b_ref, c_ref), tp)
    st_ref[...] = jnp.zeros_like(st_ref)

    @pl.when(grp == 0)
    def _dt():
        row = lax.broadcasted_iota(jnp.int32, (tp, LANES), 0)
        lane_all = lax.broadcasted_iota(jnp.int32, (tp, LANES), 1)
        dt_all = jnp.where((row < PAD) | (lane_all >= 2 * SSD_HEADS), 0.0,
                           _softplus(dt_ref[0] + dtb_ref[...]))
        dts_ref[...] = dt_all
        da_ref[...] = dt_all * (-jnp.exp(alog_ref[...]))

        def loop_c(c, carry):
            rows = _chunk_rows(c)
            da = da_ref[rows, :]
            cumf = _cumsum_rows(da)
            pk_ref[0, rows, :] = _pack3(cumf)
            pk_ref[1, rows, :] = _pack3(cumf[CHUNK - 1:CHUNK] - cumf + da)
            pk_ref[2, rows, :] = _pack2(dts_ref[rows, :])
            return carry

        lax.fori_loop(0, n_chunks, loop_c, 0)

    col_head = lax.broadcasted_iota(jnp.int32, (gw, 2 * LANES), 0) // hdim
    row_head = lax.broadcasted_iota(jnp.int32, (8 * SSD_HPG, 2 * LANES), 0) // 8
    for d in range(2):
        lane0 = SSD_HEADS * d + SSD_HPG * grp
        sel_ref[d, 0] = _selector(lane0 + col_head, 3)
        sel_ref[d, 1] = _selector(lane0 + col_head, 2)
        rsel_ref[d] = _selector(lane0 + row_head, 3)

    conv_w = jnp.concatenate([cwx_ref[...], cwb_ref[...], cwc_ref[...]], axis=1)
    conv_b = jnp.concatenate([cbx_ref[...], cbb_ref[...], cbc_ref[...]], axis=1)

    def loop_a(c, carry):
        rows = _chunk_rows(c)
        a = _conv_silu(src_ref, conv_w, conv_b, c)
        xs_ref[rows, :] = a[:, :gw]
        bc_ref[rows, :] = a[:, gw:gw + ns].astype(BF16)
        cc_ref[rows, :] = a[:, gw + ns:].astype(BF16)
        return carry

    lax.fori_loop(0, n_chunks, loop_a, 0)
    xs_ref[0:PAD, :] = jnp.zeros((PAD, gw), F32)
    bc_ref[0:PAD, :] = jnp.zeros((PAD, ns), BF16)
    cc_ref[0:PAD, :] = jnp.zeros((PAD, ns), BF16)

    eb_rows = tp // EXPAND_STEPS

    def loop_e(i, carry):
        rows = pl.ds(pl.multiple_of(i * eb_rows, CHUNK), eb_rows)
        cum4 = [_nt(pk_ref[d, rows, :], sel_ref[d, 0]) for d in range(2)]
        dt4 = [_nt(pk_ref[2, rows, :], sel_ref[d, 1]) for d in range(2)]
        xs = xs_ref[rows, :]
        for d in range(2):
            cum4_ref[d, rows, :] = cum4[d]
            xdt_ref[d, rows, :] = xs * dt4[d]
        return carry

    lax.fori_loop(0, EXPAND_STEPS, loop_e, 0)

    blk4 = lax.broadcasted_iota(jnp.int32, (CHUNK, gw), 1) // hdim
    units = [(u, d) for u in range(SCAN_UNROLL) for d in range(2)]

    def loop_p(i, carry):
        cs = [i * SCAN_UNROLL + u for u in range(SCAN_UNROLL)]
        rows = [_chunk_rows(c) for c in cs]
        cb2 = [_nt(cc_ref[r, :], jnp.concatenate([bc_ref[r, :]] * 2, axis=0)) for r in rows]
        cb4 = [jnp.concatenate([x] * (SSD_HPG // 2), axis=1) for x in cb2]
        row32 = {(u, d): _nt(rsel_ref[d], pk_ref[d, rows[u], :]) for u, d in units}
        m4, xdiag = {}, {}
        for u, d in units:
            reverse = d == 1
            cum4 = cum4_ref[d, rows[u], :]
            r32 = row32[u, d]
            row_cum = jnp.concatenate([r32[8 * r:8 * r + 8] for r in range(SSD_HPG)], axis=1)
            seg4 = cum4 - jnp.concatenate([row_cum] * (CHUNK // 8), axis=0)
            decay4 = jnp.where(_scan_tri((CHUNK, gw), reverse), jnp.exp(seg4), 0.0)
            m4[u, d] = (decay4 * cb4[u]).astype(BF16)
            xdt = xdt_ref[d, rows[u], :]
            xdt_b = xdt.astype(BF16)
            xdiag[u, d] = jnp.concatenate([jnp.where(blk4 == r, xdt_b, jnp.zeros_like(xdt_b))
                                           for r in range(SSD_HPG)], axis=0)
            cum_last = cum4[0:1] if reverse else cum4[CHUNK - 1:CHUNK]
            xw_ref[d, rows[u], :] = (jnp.exp(cum_last - cum4) * xdt).astype(BF16)
            ecl_ref[d, cs[u]] = jnp.broadcast_to(jnp.exp(cum_last), (8, gw))
        intra = {k: jnp.dot(m4[k], xdiag[k], preferred_element_type=F32) for k in units}
        for u in range(SCAN_UNROLL):
            y_ref[rows[u], :] = dsk_ref[...] * xs_ref[rows[u], :] + intra[u, 0] + intra[u, 1]
        return carry

    lax.fori_loop(0, n_chunks // SCAN_UNROLL, loop_p, 0)

    def loop_s(i, carry):
        cs, rows = _scan_steps(i, n_chunks)
        upd = {k: _tn(bc_ref[rows[k], :], xw_ref[k[1], rows[k], :]) for k in units}
        st = {(0, d): st_ref[d] for d in range(2)}
        inter = {}
        for u in range(SCAN_UNROLL):
            for d in range(2):
                inter[u, d] = jnp.dot(cc_ref[rows[u, d], :], st[u, d].astype(BF16),
                                      preferred_element_type=F32)
                st[u + 1, d] = ecl_ref[d, cs[u, d]][0:1] * st[u, d] + upd[u, d]
        for u, d in units:
            y_ref[rows[u, d], :] += jnp.exp(cum4_ref[d, rows[u, d], :]) * inter[u, d]
        for d in range(2):
            st_ref[d] = st[SCAN_UNROLL, d]
        return carry

    lax.fori_loop(0, n_chunks // SCAN_UNROLL, loop_s, 0)

    rowg = lax.broadcasted_iota(jnp.int32, (tp, gw), 0)
    y = y_ref[...] * _silu(z_ref[0].astype(F32))
    y = y * lax.rsqrt(jnp.mean(y * y, axis=1, keepdims=True) + RMS_EPS) * nw_ref[...]
    o_ref[0] = jnp.where(rowg < PAD, 0.0, y).astype(BF16)


def _ssd(proj, dt_raw, conv_w, conv_b, dt_bias, a_log, d_skip, norm_w):
    bsz, tp, _ = proj.shape
    gw, ns = SSD_GW, SSD_STATE
    nc = tp // CHUNK
    xb, bb, cb = D_INNER // gw, 2 * D_INNER // ns, 2 * D_INNER // ns + SSD_GROUPS

    def col_block(width, j0):
        return pl.BlockSpec((1, tp, width), lambda b, g: (b, 0, j0 + g))

    def par_block(nrow, width, j0):
        return pl.BlockSpec((nrow, width), lambda b, g: (0, j0 + g))

    def lane_row(v):
        v = v.astype(F32).reshape(-1)
        return jnp.pad(v, (0, LANES - v.shape[0])).reshape(1, LANES)

    cw = conv_w.astype(F32)
    cbias = conv_b.astype(F32).reshape(1, -1)
    xb_c, bb_c, cb_c = 0, D_INNER // ns, D_INNER // ns + SSD_GROUPS
    dsk = jnp.repeat(d_skip.astype(F32), SSD_HEAD_DIM).reshape(1, D_INNER)
    full_row = pl.BlockSpec((1, LANES), lambda b, g: (0, 0))
    return pl.pallas_call(
        _ssd_kernel,
        grid=(bsz, SSD_GROUPS),
        in_specs=[col_block(gw, 0), col_block(gw, xb), col_block(ns, bb), col_block(ns, cb),
                  pl.BlockSpec((1, tp, LANES), lambda b, g: (b, 0, 0)),
                  par_block(CONV_W, gw, xb_c), par_block(CONV_W, ns, bb_c), par_block(CONV_W, ns, cb_c),
                  par_block(1, gw, xb_c), par_block(1, ns, bb_c), par_block(1, ns, cb_c),
                  full_row, full_row, par_block(1, gw, 0), par_block(1, gw, 0)],
        out_specs=pl.BlockSpec((1, tp, gw), lambda b, g: (b, 0, g)),
        out_shape=jax.ShapeDtypeStruct((bsz, tp, D_INNER), BF16),
        scratch_shapes=[pltpu.VMEM((tp + 16, gw + 2 * ns), F32),
                        pltpu.VMEM((tp, gw), F32),
                        pltpu.VMEM((tp, ns), BF16),
                        pltpu.VMEM((tp, ns), BF16),
                        pltpu.VMEM((tp, LANES), F32),
                        pltpu.VMEM((tp, LANES), F32),
                        pltpu.VMEM((2, 2, gw, 2 * LANES), BF16),
                        pltpu.VMEM((2, 8 * SSD_HPG, 2 * LANES), BF16),
                        pltpu.VMEM((3, tp, 2 * LANES), BF16),
                        pltpu.VMEM((2, tp, gw), F32),
                        pltpu.VMEM((2, tp, gw), F32),
                        pltpu.VMEM((2, tp, gw), BF16),
                        pltpu.VMEM((2, nc, 8, gw), F32),
                        pltpu.VMEM((tp, gw), F32),
                        pltpu.VMEM((2, ns, gw), F32)],
        compiler_params=_cparams("parallel", "arbitrary"),
    )(proj, proj, proj, proj, dt_raw, cw, cw, cw, cbias, cbias, cbias,
      lane_row(dt_bias), lane_row(a_log), dsk, norm_w.astype(F32).reshape(1, -1))


def _even_mixer(h2, bsz, tp, norm_w, w_in, rpb, conv_w, conv_b, gate_bias, ml_norm_w, w_out):
    proj, gates = _norm_proj(h2, norm_w, w_in, EV_MAIN)
    proj = proj.reshape(bsz, tp, EV_MAIN)
    y_a = _natten(proj, _na_bias_table(rpb))
    y_b = _mlstm(proj, gates.reshape(bsz, tp, LANES), conv_w, conv_b, gate_bias, ml_norm_w)
    return _out_proj([y_a.reshape(bsz * tp, D_A), y_b.reshape(bsz * tp, D_B)],
                     [w_out[:D_A], w_out[D_A:]], h2)


def _odd_mixer(h2, bsz, tp, norm_w, w_in, conv_w, conv_b, dt_bias, a_log, d_skip, ssd_norm_w, w_out):
    proj, dt_raw = _norm_proj(h2, norm_w, w_in, OD_MAIN)
    y = _ssd(proj.reshape(bsz, tp, OD_MAIN), dt_raw.reshape(bsz, tp, LANES), conv_w, conv_b,
             dt_bias, a_log, d_skip, ssd_norm_w)
    return _out_proj([y.reshape(bsz * tp, D_INNER)], [w_out], h2)


def kernel(x, meta_tokens, ffn1_norm, ffn1_gate, ffn1_up, ffn1_down, mix_norm, ffn2_norm, ffn2_gate, ffn2_up, ffn2_down, ev_w_in, na_rpb, ml_conv_w, ml_conv_b, ml_gate_bias, ml_norm_w, ev_w_out, od_w_in, ssd_conv_w, ssd_conv_b, ssd_dt_bias, ssd_a_log, ssd_d, ssd_norm_w, od_w_out, final_norm):
    bsz, seq, _ = x.shape
    tp = PAD + N_META + seq
    assert x.shape[2] == D_MODEL and seq % GRID_W == 0 and tp % ROW_TILE == 0
    assert seq // GRID_W >= NA_WIN_H
    head = jnp.concatenate([jnp.zeros((PAD, D_MODEL), F32), meta_tokens.astype(F32)], axis=0)
    h = jnp.concatenate([jnp.broadcast_to(head[None], (bsz, PAD + N_META, D_MODEL)), x], axis=1)
    h2 = h.reshape(bsz * tp, D_MODEL)
    n_layers = ffn1_norm.shape[0]
    for layer in range(n_layers):
        h2 = _ffn(h2, ffn1_norm[layer], ffn1_gate[layer], ffn1_up[layer], ffn1_down[layer])
        i = layer // 2
        if layer % 2 == 0:
            h2 = _even_mixer(h2, bsz, tp, mix_norm[layer], ev_w_in[i], na_rpb[i], ml_conv_w[i],
                             ml_conv_b[i], ml_gate_bias[i], ml_norm_w[i], ev_w_out[i])
        else:
            h2 = _odd_mixer(h2, bsz, tp, mix_norm[layer], od_w_in[i], ssd_conv_w[i], ssd_conv_b[i],
                            ssd_dt_bias[i], ssd_a_log[i], ssd_d[i], ssd_norm_w[i], od_w_out[i])
        if layer + 1 < n_layers:
            h2 = _ffn(h2, ffn2_norm[layer], ffn2_gate[layer], ffn2_up[layer], ffn2_down[layer])
    last = n_layers - 1
    return _ffn_final(h2.reshape(bsz, tp, D_MODEL), ffn2_norm[last], ffn2_gate[last], ffn2_up[last],
                      ffn2_down[last], final_norm, seq)
```
